```python
import jax, jax.numpy as jnp
from jax import lax
import numpy as np

D_MODEL = 1024
BATCH = 1
SEQ = 16384
DEPTH = 1
DEC_BATCH = 2
DEC_SEQ = 8192
PAST_LEN = 128

N_MEM = 256
GRID_W = 64
EPS = 1e-6

GLA_HEADS = 4
GLA_DK = 128
GLA_DV = 256
GLA_RANK = 16
GLA_TAU = 16.0
GLA_CHUNK = 64
GLA_QK = GLA_HEADS * GLA_DK
GLA_V = GLA_HEADS * GLA_DV

NAT_HEADS = 8
NAT_DH = 64
NAT_KH = 8
NAT_KW = 16
NAT_W = NAT_HEADS * NAT_DH

MEM_HEADS = 4
MEM_DH = 128
MEM_W = MEM_HEADS * MEM_DH

MIX_W = GLA_V + NAT_W + MEM_W
SPLIT_SIZES = (GLA_QK, GLA_QK, GLA_V, GLA_V, GLA_RANK, GLA_RANK,
               NAT_W, NAT_W, NAT_W, NAT_W, MEM_W, MEM_W)
IN_W = 6176

kernel_name = "hybrid_gla_natten_mem_encoder"


def rms_norm(x, g):
    xf = x.astype(jnp.float32)
    y = xf * lax.rsqrt(jnp.mean(xf * xf, axis=-1, keepdims=True) + EPS)
    return (y * g.astype(jnp.float32)).astype(x.dtype)


def _gla_scan(q, k, v, log_a):
    B, N, H, DK = q.shape
    DV = v.shape[-1]
    nc = N // GLA_CHUNK

    def to_chunks(t):
        t = t.astype(jnp.float32).reshape(B, nc, GLA_CHUNK, H, t.shape[-1])
        return jnp.moveaxis(t, 1, 0)

    qc, kc, vc, ac = to_chunks(q), to_chunks(k), to_chunks(v), to_chunks(log_a)
    causal = jnp.tril(jnp.ones((GLA_CHUNK, GLA_CHUNK), dtype=bool))[None, :, :, None, None]

    def step(S, inp):
        qi, ki, vi, ai = inp
        b = jnp.cumsum(ai, axis=1)
        diff = jnp.where(causal, b[:, :, None] - b[:, None, :], -jnp.inf)
        att = jnp.einsum('bihd,bjhd,bijhd->bhij', qi, ki, jnp.exp(diff))
        o_intra = jnp.einsum('bhij,bjhv->bihv', att, vi)
        o_inter = jnp.einsum('bihd,bhdv->bihv', qi * jnp.exp(b), S)
        b_last = b[:, -1]
        k_dec = ki * jnp.exp(b_last[:, None] - b)
        S_new = jnp.exp(b_last)[..., None] * S + jnp.einsum('bjhd,bjhv->bhdv', k_dec, vi)
        return S_new, o_intra + o_inter

    S0 = jnp.zeros((B, H, DK, DV), jnp.float32)
    _, o = lax.scan(step, S0, (qc, kc, vc, ac))
    return jnp.moveaxis(o, 0, 1).reshape(B, N, H, DV)


def gla_branch(q, k, v, g, lr_f, lr_b, w_f, b_f, w_b, b_b, norm_g):
    B, N, _ = q.shape
    log_a_f = jax.nn.log_sigmoid((lr_f @ w_f + b_f).astype(jnp.float32)) / GLA_TAU
    log_a_b = jax.nn.log_sigmoid((lr_b @ w_b + b_b).astype(jnp.float32)) / GLA_TAU
    qh = q.reshape(B, N, GLA_HEADS, GLA_DK) * (GLA_DK ** -0.5)
    kh = k.reshape(B, N, GLA_HEADS, GLA_DK)
    vh = v.reshape(B, N, GLA_HEADS, GLA_DV)
    af = log_a_f.reshape(B, N, GLA_HEADS, GLA_DK)
    ab = log_a_b.reshape(B, N, GLA_HEADS, GLA_DK)
    o_f = _gla_scan(qh, kh, vh, af)
    o_b = _gla_scan(qh[:, ::-1], kh[:, ::-1], vh[:, ::-1], ab[:, ::-1])[:, ::-1]
    o = rms_norm(o_f + o_b, norm_g)
    o = o.reshape(B, N, GLA_V) * jax.nn.silu(g.astype(jnp.float32))
    return o.astype(q.dtype)


def nat_branch(q, k, v, g, rpb):
    B, N, _ = q.shape
    rows = N // GRID_W
    kh = min(NAT_KH, rows)
    shp = (B, rows, GRID_W, NAT_HEADS, NAT_DH)
    qg = q.reshape(shp) * (NAT_DH ** -0.5)
    kg_all = k.reshape(shp)
    vg_all = v.reshape(shp)
    r = jnp.arange(rows)
    row_start = jnp.clip(r - kh // 2, 0, rows - kh)
    row_idx = row_start[:, None] + jnp.arange(kh)[None, :]
    kg = kg_all[:, row_idx]
    vg = vg_all[:, row_idx]
    c = jnp.arange(GRID_W)
    col_start = jnp.clip(c - NAT_KW // 2, 0, GRID_W - NAT_KW)
    col_in = (c[None, :] >= col_start[:, None]) & (c[None, :] < col_start[:, None] + NAT_KW)
    drow = row_idx - r[:, None] + (NAT_KH - 1)
    dcol = jnp.clip(c[None, :] - c[:, None] + (NAT_KW - 1), 0, 2 * NAT_KW - 2)
    bias = rpb[:, drow[:, None, :, None], dcol[None, :, None, :]]
    s = jnp.einsum('brqhd,brkwhd->bhrqkw', qg, kg).astype(jnp.float32)
    s = jnp.where(col_in[:, None, :], s + bias.astype(jnp.float32), -jnp.inf)
    p = jax.nn.softmax(s, axis=(-2, -1)).astype(v.dtype)
    o = jnp.einsum('bhrqkw,brkwhd->brqhd', p, vg).reshape(B, N, NAT_W)
    return (o.astype(jnp.float32) * jax.nn.silu(g.astype(jnp.float32))).astype(q.dtype)


def mem_branch(q, g, mem, mem_norm_g, w_mem_kv):
    B, N, _ = q.shape
    m = rms_norm(mem, mem_norm_g)
    mk, mv = jnp.split(m @ w_mem_kv, 2, axis=-1)
    M = mem.shape[1]
    qh = q.reshape(B, N, MEM_HEADS, MEM_DH) * (MEM_DH ** -0.5)
    mk = mk.reshape(B, M, MEM_HEADS, MEM_DH)
    mv = mv.reshape(B, M, MEM_HEADS, MEM_DH)
    s = jnp.einsum('bnhd,bmhd->bhnm', qh, mk).astype(jnp.float32)
    p = jax.nn.softmax(s, axis=-1).astype(mv.dtype)
    o = jnp.einsum('bhnm,bmhd->bnhd', p, mv).reshape(B, N, MEM_W)
    return (o.astype(jnp.float32) * jax.nn.silu(g.astype(jnp.float32))).astype(q.dtype)


def encoder_layer(x, mem, pre_g, w_in, gw_f, gb_f, gw_b, gb_b, gla_ng, rpb,
                  mem_ng, w_mem_kv, w_out, post_g):
    h = rms_norm(x, pre_g)
    proj = h @ w_in
    offs = np.cumsum(SPLIT_SIZES)[:-1].tolist()
    (gq, gk, gv, gg, glr_f, glr_b, nq, nk, nv, ng, mq, mg) = jnp.split(proj, offs, axis=-1)
    o_gla = gla_branch(gq, gk, gv, gg, glr_f, glr_b, gw_f, gb_f, gw_b, gb_b, gla_ng)
    o_nat = nat_branch(nq, nk, nv, ng, rpb)
    o_mem = mem_branch(mq, mg, mem, mem_ng, w_mem_kv)
    mixed = jnp.concatenate([o_gla, o_nat, o_mem], axis=-1)
    out = mixed @ w_out
    return (x + rms_norm(out, post_g)).astype(x.dtype)


def run_trunk(x, mem, pre_norm_g, w_in, gla_w_fwd, gla_b_fwd, gla_w_bwd, gla_b_bwd,
              gla_norm_g, nat_rpb, mem_norm_g, w_mem_kv, w_out, post_norm_g):
    for l in range(DEPTH):
        x = encoder_layer(x, mem, pre_norm_g[l], w_in[l], gla_w_fwd[l], gla_b_fwd[l],
                          gla_w_bwd[l], gla_b_bwd[l], gla_norm_g[l], nat_rpb[l],
                          mem_norm_g[l], w_mem_kv[l], w_out[l], post_norm_g[l])
    return x


def setup_inputs(seed: int = 0) -> dict:
    key = jax.random.key(seed)
    ks = jax.random.split(key, 20)
    f32 = jnp.float32
    nrm = lambda k, s, sc: jax.random.normal(k, s, f32) * sc
    return {
        "x_prompt": nrm(ks[0], (BATCH, SEQ, D_MODEL), 1.0),
        "x_sample": nrm(ks[1], (DEC_BATCH, DEC_SEQ, D_MODEL), 1.0),
        "mem_prompt": nrm(ks[2], (BATCH, N_MEM, D_MODEL), 1.0),
        "mem_sample": nrm(ks[3], (DEC_BATCH, N_MEM, D_MODEL), 1.0),
        "pre_norm_g": 1.0 + nrm(ks[4], (DEPTH, D_MODEL), 0.01),
        "w_in": nrm(ks[5], (DEPTH, D_MODEL, IN_W), D_MODEL ** -0.5),
        "gla_w_fwd": nrm(ks[6], (DEPTH, GLA_RANK, GLA_QK), GLA_RANK ** -0.5),
        "gla_b_fwd": nrm(ks[7], (DEPTH, GLA_QK), 0.1),
        "gla_w_bwd": nrm(ks[8], (DEPTH, GLA_RANK, GLA_QK), GLA_RANK ** -0.5),
        "gla_b_bwd": nrm(ks[9], (DEPTH, GLA_QK), 0.1),
        "gla_norm_g": 1.0 + nrm(ks[10], (DEPTH, GLA_DV), 0.01),
        "nat_rpb": nrm(ks[11], (DEPTH, NAT_HEADS, 2 * NAT_KH - 1, 2 * NAT_KW - 1), 0.02),
        "mem_norm_g": 1.0 + nrm(ks[12], (DEPTH, D_MODEL), 0.01),
        "w_mem_kv": nrm(ks[13], (DEPTH, D_MODEL, 2 * MEM_W), D_MODEL ** -0.5),
        "w_out": nrm(ks[14], (DEPTH, MIX_W, D_MODEL), MIX_W ** -0.5),
        "post_norm_g": 1.0 + nrm(ks[15], (DEPTH, D_MODEL), 0.01),
    }


def reference(x_prompt, x_sample, mem_prompt, mem_sample, pre_norm_g, w_in,
              gla_w_fwd, gla_b_fwd, gla_w_bwd, gla_b_bwd, gla_norm_g, nat_rpb,
              mem_norm_g, w_mem_kv, w_out, post_norm_g):
    y_prompt = run_trunk(x_prompt, mem_prompt, pre_norm_g, w_in, gla_w_fwd, gla_b_fwd,
                         gla_w_bwd, gla_b_bwd, gla_norm_g, nat_rpb, mem_norm_g,
                         w_mem_kv, w_out, post_norm_g)
    y_sample = run_trunk(x_sample, mem_sample, pre_norm_g, w_in, gla_w_fwd, gla_b_fwd,
                         gla_w_bwd, gla_b_bwd, gla_norm_g, nat_rpb, mem_norm_g,
                         w_mem_kv, w_out, post_norm_g)
    return (y_prompt, y_sample)
```

```python
import functools

import numpy as np
import jax
import jax.numpy as jnp
from jax import lax
from jax.experimental import pallas as pl
from jax.experimental.pallas import tpu as pltpu

F32 = jnp.float32
BF16 = jnp.bfloat16

D_MODEL = 1024
N_MEM = 256
GRID_W = 64
EPS = 1e-6

GLA_HEADS = 4
GLA_DK = 128
GLA_DV = 256
GLA_RANK = 16
GLA_TAU = 16.0
GLA_QK = GLA_HEADS * GLA_DK
GLA_V = GLA_HEADS * GLA_DV

NAT_HEADS = 8
NAT_DH = 64
NAT_KH = 8
NAT_KW = 16
NAT_W = NAT_HEADS * NAT_DH
NAT_PAIRS = NAT_HEADS // 2

MEM_HEADS = 4
MEM_DH = 128
MEM_W = MEM_HEADS * MEM_DH

MIX_W = GLA_V + NAT_W + MEM_W

COL_BLK = 512
P_GQ, P_GK, P_GV, P_GG = 0, 512, 1024, 2048
P_NQ, P_NK, P_NV, P_NG = 3072, 3584, 4096, 4608
P_MQ, P_MG = 5120, 5632
P_LR = 6144
LR_PAD = 128
P_W = P_LR + LR_PAD

NEG_BIG = -1e30

VMEM_LIMIT = 56 * 1024 * 1024


def _cparams(n_axes):
    return pltpu.CompilerParams(
        dimension_semantics=("arbitrary",) * n_axes,
        vmem_limit_bytes=VMEM_LIMIT,
    )


def _resident(shape, index_map):
    return pl.BlockSpec(shape, index_map, pipeline_mode=pl.Buffered(1))


def _silu(g):
    return g * jax.nn.sigmoid(g)


def _inproj_kernel(x_ref, g_ref, w_ref, o_ref, *, scales):
    x = x_ref[...]
    ms = jnp.mean(x * x, axis=-1, keepdims=True)
    h = ((x * lax.rsqrt(ms + EPS)) * g_ref[...]).astype(BF16)
    for c0, c1, scale in scales:
        acc = jnp.dot(h, w_ref[:, c0:c1], preferred_element_type=F32)
        if scale is not None:
            acc = acc * scale
        o_ref[:, c0:c1] = acc.astype(BF16)


def _inproj(x2, pre_g, w_in_p, tm):
    rows = x2.shape[0]
    scales = []
    for c0 in range(0, P_LR, COL_BLK):
        scale = None
        if c0 == P_GQ:
            scale = GLA_DK ** -0.5
        elif c0 == P_NQ:
            scale = NAT_DH ** -0.5
        elif c0 == P_MQ:
            scale = MEM_DH ** -0.5
        scales.append((c0, c0 + COL_BLK, scale))
    scales.append((P_LR, P_W, None))
    return pl.pallas_call(
        functools.partial(_inproj_kernel, scales=tuple(scales)),
        grid=(rows // tm,),
        in_specs=[
            pl.BlockSpec((tm, D_MODEL), lambda i: (i, 0)),
            _resident((1, D_MODEL), lambda i: (0, 0)),
            _resident((D_MODEL, P_W), lambda i: (0, 0)),
        ],
        out_specs=pl.BlockSpec((tm, P_W), lambda i: (i, 0)),
        out_shape=jax.ShapeDtypeStruct((rows, P_W), BF16),
        compiler_params=_cparams(1),
        name="inproj",
    )(x2, pre_g, w_in_p)


def _block_row_bcast(b, blk, ridx):
    t = b.shape[0]
    if blk >= 8:
        pieces = []
        for b0 in range(0, t, blk):
            row = b[b0 + ridx:b0 + ridx + 1, :]
            pieces.append(jnp.broadcast_to(row, (blk, b.shape[1])))
        return pieces[0] if len(pieces) == 1 else jnp.concatenate(pieces, axis=0)
    m = lax.broadcasted_iota(jnp.int32, b.shape, 0) & (blk - 1)
    out = b
    for pos in range(blk):
        if pos == ridx:
            continue
        shifted = pltpu.roll(b, (pos - ridx) % t, axis=0)
        out = jnp.where(m == pos, shifted, out)
    return out


def _log_sigmoid(z):
    return jnp.minimum(z, 0.0) - jnp.log1p(jnp.exp(-jnp.abs(z)))


def _gla_direction(q_ref, k_ref, v_ref, lr_ref, wdec_ref, bdec_ref, o_ref, s_ref, *, reverse, t):
    d = 1 if reverse else 0
    q_all = q_ref[...]
    k_all = k_ref[...]
    z = jnp.dot(lr_ref[...], wdec_ref[:, d * GLA_QK:(d + 1) * GLA_QK], preferred_element_type=F32)
    z = z + bdec_ref[:, d * GLA_QK:(d + 1) * GLA_QK]
    la = _log_sigmoid(z) * (1.0 / GLA_TAU)

    ri = lax.broadcasted_iota(jnp.int32, (t, t), 0)
    ci = lax.broadcasted_iota(jnp.int32, (t, t), 1)
    tri = jnp.where((ci >= ri) if reverse else (ci <= ri), 1.0, 0.0).astype(BF16)
    la_hi = la.astype(BF16)
    la_lo = (la - la_hi.astype(F32)).astype(BF16)
    b = (jnp.dot(tri, la_hi, preferred_element_type=F32)
         + jnp.dot(tri, la_lo, preferred_element_type=F32))
    b_last = b[0:1, :] if reverse else b[t - 1:t, :]

    e_in = jnp.exp(b)
    e_out = jnp.exp(b_last - b)
    e_all = jnp.exp(b_last)

    rowi = lax.broadcasted_iota(jnp.int32, (t, 1), 0)
    xor_rc = ri ^ ci
    levels = []
    s = t // 2
    while s >= 1:
        blk = 2 * s
        ridx = s if reverse else s - 1
        e_l = jnp.exp(-jnp.abs(b - _block_row_bcast(b, blk, ridx)))
        later = (rowi & (blk - 1)) >= s
        is_query = jnp.logical_not(later) if reverse else later
        levels.append((blk, e_l, is_query))
        s //= 2

    for h in range(GLA_HEADS):
        kl = slice(h * GLA_DK, (h + 1) * GLA_DK)
        vl = slice(h * GLA_DV, (h + 1) * GLA_DV)
        qh = q_all[:, kl].astype(F32)
        kh = k_all[:, kl].astype(F32)
        vh = v_ref[:, vl]

        att = jnp.where(ri == ci, jnp.sum(qh * kh, axis=-1, keepdims=True), 0.0)
        for blk, e_l, is_query in levels:
            eh = e_l[:, kl]
            qt = jnp.where(is_query, qh * eh, 0.0).astype(BF16)
            kt = jnp.where(is_query, 0.0, kh * eh).astype(BF16)
            p = lax.dot_general(qt, kt, (((1,), (1,)), ((), ())), preferred_element_type=F32)
            if blk < t:
                p = jnp.where(xor_rc < blk, p, 0.0)
            att = att + p
        o = jnp.dot(att.astype(BF16), vh, preferred_element_type=F32)

        st = s_ref[h]
        q_in = (qh * e_in[:, kl]).astype(BF16)
        o = o + lax.dot_general(q_in, st.astype(BF16), (((1,), (1,)), ((), ())),
                                preferred_element_type=F32)
        o_ref[:, vl] = o.astype(o_ref.dtype)

        k_out = (kh * e_out[:, kl]).astype(BF16)
        upd = lax.dot_general(vh, k_out, (((0,), (0,)), ((), ())), preferred_element_type=F32)
        s_ref[h] = st * e_all[:, kl] + upd


def _gla_kernel(qf, kf, vf, lf, qb, kb, vb, lb, wdec, bdec, of, ob, sf, sb, *, t):
    @pl.when(pl.program_id(1) == 0)
    def _():
        sf[...] = jnp.zeros_like(sf)
        sb[...] = jnp.zeros_like(sb)

    _gla_direction(qf, kf, vf, lf, wdec, bdec, of, sf, reverse=False, t=t)
    _gla_direction(qb, kb, vb, lb, wdec, bdec, ob, sb, reverse=True, t=t)


def _gla(p, wdec, bdec, n_seq, n_tok, t):
    nt = n_tok // t
    rows = n_seq * n_tok

    def fwd(col):
        return lambda s, i: (s * nt + i, col)

    def bwd(col):
        return lambda s, i: (s * nt + nt - 1 - i, col)

    def specs(m):
        return [
            pl.BlockSpec((t, GLA_QK), m(P_GQ // GLA_QK)),
            pl.BlockSpec((t, GLA_QK), m(P_GK // GLA_QK)),
            pl.BlockSpec((t, GLA_V), m(P_GV // GLA_V)),
            pl.BlockSpec((t, LR_PAD), m(P_LR // LR_PAD)),
        ]

    out_sds = jax.ShapeDtypeStruct((rows, GLA_V), BF16)
    return pl.pallas_call(
        functools.partial(_gla_kernel, t=t),
        grid=(n_seq, nt),
        in_specs=specs(fwd) + specs(bwd) + [
            _resident((LR_PAD, 2 * GLA_QK), lambda s, i: (0, 0)),
            _resident((1, 2 * GLA_QK), lambda s, i: (0, 0)),
        ],
        out_specs=[pl.BlockSpec((t, GLA_V), fwd(0)), pl.BlockSpec((t, GLA_V), bwd(0))],
        out_shape=[out_sds, out_sds],
        scratch_shapes=[pltpu.VMEM((GLA_HEADS, GLA_DV, GLA_DK), F32),
                        pltpu.VMEM((GLA_HEADS, GLA_DV, GLA_DK), F32)],
        compiler_params=_cparams(2),
        name="gla",
    )(p, p, p, p, p, p, p, p, wdec, bdec)


NAT_TILE_ROWS = 8
NAT_TILE = NAT_TILE_ROWS * GRID_W
NAT_KEYS = NAT_KH * GRID_W


def _nat_bias_table(rpb):
    cq = np.arange(GRID_W)[:, None]
    ck = np.arange(GRID_W)[None, :]
    cs = np.clip(cq - NAT_KW // 2, 0, GRID_W - NAT_KW)
    col_in = (ck >= cs) & (ck < cs + NAT_KW)
    dcol = np.clip(ck - cq + (NAT_KW - 1), 0, 2 * NAT_KW - 2)
    drow = np.arange(NAT_KH)[:, None] + np.arange(NAT_KH)[None, :]
    tab = rpb[:, drow[:, None, :, None], dcol[None, :, None, :]]
    tab = jnp.where(col_in[None, None, :, None, :], tab.astype(F32), NEG_BIG)
    tab = tab.reshape(NAT_PAIRS, 2, NAT_KH, GRID_W, NAT_KEYS)
    return tab.transpose(0, 2, 1, 3, 4).reshape(NAT_PAIRS, NAT_KH, 2 * GRID_W, NAT_KEYS)


def _nat_kernel(q_ref, kp_ref, kc_ref, kn_ref, vp_ref, vc_ref, vn_ref, g_ref, bias_ref, o_ref,
                kbuf, vbuf, *, n_rows):
    i = pl.program_id(1)
    kbuf[0:NAT_TILE, :] = kp_ref[...]
    kbuf[NAT_TILE:2 * NAT_TILE, :] = kc_ref[...]
    kbuf[2 * NAT_TILE:3 * NAT_TILE, :] = kn_ref[...]
    vbuf[0:NAT_TILE, :] = vp_ref[...]
    vbuf[NAT_TILE:2 * NAT_TILE, :] = vc_ref[...]
    vbuf[2 * NAT_TILE:3 * NAT_TILE, :] = vn_ref[...]

    lane = lax.broadcasted_iota(jnp.int32, (GRID_W, 2 * NAT_DH), 1)
    first = lane < NAT_DH
    r0 = i * NAT_TILE_ROWS
    for rr in range(NAT_TILE_ROWS):
        r = r0 + rr
        rs = jnp.clip(r - NAT_KH // 2, 0, n_rows - NAT_KH)
        d0 = rs - r + (NAT_KH - 1)
        off = pl.multiple_of((rs - r0 + NAT_TILE_ROWS) * GRID_W, GRID_W)
        rows = slice(rr * GRID_W, (rr + 1) * GRID_W)
        for p in range(NAT_PAIRS):
            lanes = slice(p * 2 * NAT_DH, (p + 1) * 2 * NAT_DH)
            qp = q_ref[rows, lanes]
            zero = jnp.zeros_like(qp)
            q2 = jnp.concatenate([jnp.where(first, qp, zero), jnp.where(first, zero, qp)], axis=0)
            kk = kbuf[pl.ds(off, NAT_KEYS), lanes]
            vv = vbuf[pl.ds(off, NAT_KEYS), lanes]
            s = lax.dot_general(q2, kk, (((1,), (1,)), ((), ())), preferred_element_type=F32)
            s = s + bias_ref[p, d0]
            m = jnp.max(s, axis=-1, keepdims=True)
            e = jnp.exp(s - m)
            l = jnp.sum(e, axis=-1, keepdims=True)
            pv = jnp.dot(e.astype(BF16), vv, preferred_element_type=F32) / l
            o = jnp.where(first, pv[0:GRID_W, :], pv[GRID_W:2 * GRID_W, :])
            g = g_ref[rows, lanes].astype(F32)
            o_ref[rows, lanes] = (o * _silu(g)).astype(o_ref.dtype)


def _nat(p, bias_tab, n_seq, n_tok):
    nt = n_tok // NAT_TILE
    n_rows = n_tok // GRID_W
    rows = n_seq * n_tok
    qcol, kcol, vcol, gcol = (c // NAT_W for c in (P_NQ, P_NK, P_NV, P_NG))

    def at(col, shift):
        def index(s, i):
            return (s * nt + jnp.clip(i + shift, 0, nt - 1), col)
        return index

    blk = (NAT_TILE, NAT_W)
    return pl.pallas_call(
        functools.partial(_nat_kernel, n_rows=n_rows),
        grid=(n_seq, nt),
        in_specs=[
            pl.BlockSpec(blk, at(qcol, 0)),
            pl.BlockSpec(blk, at(kcol, -1)), pl.BlockSpec(blk, at(kcol, 0)), pl.BlockSpec(blk, at(kcol, 1)),
            pl.BlockSpec(blk, at(vcol, -1)), pl.BlockSpec(blk, at(vcol, 0)), pl.BlockSpec(blk, at(vcol, 1)),
            pl.BlockSpec(blk, at(gcol, 0)),
            _resident(bias_tab.shape, lambda s, i: (0, 0, 0, 0)),
        ],
        out_specs=pl.BlockSpec(blk, at(0, 0)),
        out_shape=jax.ShapeDtypeStruct((rows, NAT_W), BF16),
        scratch_shapes=[pltpu.VMEM((3 * NAT_TILE, NAT_W), BF16), pltpu.VMEM((3 * NAT_TILE, NAT_W), BF16)],
        compiler_params=_cparams(2),
        name="nat",
    )(p, p, p, p, p, p, p, p, bias_tab)


def _memkv_kernel(m_ref, g_ref, w_ref, o_ref):
    x = m_ref[...]
    ms = jnp.mean(x * x, axis=-1, keepdims=True)
    h = ((x * lax.rsqrt(ms + EPS)) * g_ref[...]).astype(BF16)
    o_ref[...] = jnp.dot(h, w_ref[...], preferred_element_type=F32).astype(o_ref.dtype)


def _memkv(mem2, mem_g, w_kv):
    rows = mem2.shape[0]
    return pl.pallas_call(
        _memkv_kernel,
        grid=(rows // N_MEM,),
        in_specs=[
            pl.BlockSpec((N_MEM, D_MODEL), lambda i: (i, 0)),
            _resident((1, D_MODEL), lambda i: (0, 0)),
            _resident((D_MODEL, 2 * MEM_W), lambda i: (0, 0)),
        ],
        out_specs=pl.BlockSpec((N_MEM, 2 * MEM_W), lambda i: (i, 0)),
        out_shape=jax.ShapeDtypeStruct((rows, 2 * MEM_W), BF16),
        compiler_params=_cparams(1),
        name="memkv",
    )(mem2, mem_g, w_kv)


def _memattn_kernel(q_ref, g_ref, kv_ref, o_ref):
    for h in range(MEM_HEADS):
        lanes = slice(h * MEM_DH, (h + 1) * MEM_DH)
        kh = kv_ref[:, lanes]
        vh = kv_ref[:, MEM_W + h * MEM_DH:MEM_W + (h + 1) * MEM_DH]
        s = lax.dot_general(q_ref[:, lanes], kh, (((1,), (1,)), ((), ())), preferred_element_type=F32)
        m = jnp.max(s, axis=-1, keepdims=True)
        e = jnp.exp(s - m)
        l = jnp.sum(e, axis=-1, keepdims=True)
        o = jnp.dot(e.astype(BF16), vh, preferred_element_type=F32) / l
        g = g_ref[:, lanes].astype(F32)
        o_ref[:, lanes] = (o * _silu(g)).astype(o_ref.dtype)


def _memattn(p, mkv, n_seq, n_tok, tq):
    nt = n_tok // tq
    rows = n_seq * n_tok
    return pl.pallas_call(
        _memattn_kernel,
        grid=(n_seq, nt),
        in_specs=[
            pl.BlockSpec((tq, MEM_W), lambda s, i: (s * nt + i, P_MQ // MEM_W)),
            pl.BlockSpec((tq, MEM_W), lambda s, i: (s * nt + i, P_MG // MEM_W)),
            pl.BlockSpec((N_MEM, 2 * MEM_W), lambda s, i: (s, 0)),
        ],
        out_specs=pl.BlockSpec((tq, MEM_W), lambda s, i: (s * nt + i, 0)),
        out_shape=jax.ShapeDtypeStruct((rows, MEM_W), BF16),
        compiler_params=_cparams(2),
        name="memattn",
    )(p, p, mkv)


def _out_kernel(of_ref, ob_ref, gg_ref, nat_ref, mem_ref, x_ref, w_ref, gng_ref, post_ref, y_ref):
    acc = jnp.dot(nat_ref[...], w_ref[GLA_V:GLA_V + NAT_W, :], preferred_element_type=F32)
    acc = acc + jnp.dot(mem_ref[...], w_ref[GLA_V + NAT_W:MIX_W, :], preferred_element_type=F32)
    for h in range(GLA_HEADS):
        lanes = slice(h * GLA_DV, (h + 1) * GLA_DV)
        o = of_ref[:, lanes].astype(F32) + ob_ref[:, lanes].astype(F32)
        ms = jnp.mean(o * o, axis=-1, keepdims=True)
        on = (o * lax.rsqrt(ms + EPS)) * gng_ref[...]
        og = (on * _silu(gg_ref[:, lanes].astype(F32))).astype(BF16)
        acc = acc + jnp.dot(og, w_ref[lanes, :], preferred_element_type=F32)
    ms = jnp.mean(acc * acc, axis=-1, keepdims=True)
    y_ref[...] = x_ref[...] + (acc * lax.rsqrt(ms + EPS)) * post_ref[...]


def _out(o_f, o_b, p, o_nat, o_mem, x2, w_out, gla_ng, post_g, tm):
    rows = x2.shape[0]
    return pl.pallas_call(
        _out_kernel,
        grid=(rows // tm,),
        in_specs=[
            pl.BlockSpec((tm, GLA_V), lambda i: (i, 0)),
            pl.BlockSpec((tm, GLA_V), lambda i: (i, 0)),
            pl.BlockSpec((tm, GLA_V), lambda i: (i, P_GG // GLA_V)),
            pl.BlockSpec((tm, NAT_W), lambda i: (i, 0)),
            pl.BlockSpec((tm, MEM_W), lambda i: (i, 0)),
            pl.BlockSpec((tm, D_MODEL), lambda i: (i, 0)),
            _resident((MIX_W, D_MODEL), lambda i: (0, 0)),
            _resident((1, GLA_DV), lambda i: (0, 0)),
            _resident((1, D_MODEL), lambda i: (0, 0)),
        ],
        out_specs=pl.BlockSpec((tm, D_MODEL), lambda i: (i, 0)),
        out_shape=jax.ShapeDtypeStruct((rows, D_MODEL), F32),
        compiler_params=_cparams(1),
        name="outproj",
    )(o_f, o_b, p, o_nat, o_mem, x2, w_out, gla_ng, post_g)


def _choose_tile(n, pref):
    t = pref
    while n % t:
        t //= 2
    return t


def _prepare_weights(w_in, gw_f, gb_f, gw_b, gb_b, w_mem_kv, w_out):
    offs = np.cumsum([0, GLA_QK, GLA_QK, GLA_V, GLA_V, GLA_RANK, GLA_RANK,
                      NAT_W, NAT_W, NAT_W, NAT_W, MEM_W, MEM_W])
    seg = [w_in[:, offs[k]:offs[k + 1]] for k in range(12)]
    gq, gk, gv, gg, lrf, lrb, nq, nk, nv, ng, mq, mg = seg
    pad = jnp.zeros((D_MODEL, LR_PAD - 2 * GLA_RANK), w_in.dtype)
    w_in_p = jnp.concatenate([gq, gk, gv, gg, nq, nk, nv, ng, mq, mg, lrf, lrb, pad], axis=1).astype(BF16)
    wdec = jnp.zeros((LR_PAD, 2 * GLA_QK), F32)
    wdec = wdec.at[0:GLA_RANK, 0:GLA_QK].set(gw_f)
    wdec = wdec.at[GLA_RANK:2 * GLA_RANK, GLA_QK:].set(gw_b)
    bdec = jnp.concatenate([gb_f, gb_b]).reshape(1, 2 * GLA_QK).astype(F32)
    return w_in_p, wdec.astype(BF16), bdec, w_mem_kv.astype(BF16), w_out.astype(BF16)


def _trunk(x, mkv, pre_g, w_in_p, wdec, bdec, gla_ng, bias_tab, w_out, post_g):
    n_seq, n_tok, _ = x.shape
    rows = n_seq * n_tok
    x2 = x.reshape(rows, D_MODEL)
    tm = _choose_tile(rows, 512)
    p = _inproj(x2, pre_g, w_in_p, tm)
    o_f, o_b = _gla(p, wdec, bdec, n_seq, n_tok, _choose_tile(n_tok, 256))
    o_nat = _nat(p, bias_tab, n_seq, n_tok)
    o_mem = _memattn(p, mkv, n_seq, n_tok, _choose_tile(n_tok, 512))
    y = _out(o_f, o_b, p, o_nat, o_mem, x2, w_out, gla_ng, post_g, tm)
    return y.reshape(x.shape)


def kernel(x_prompt, x_sample, mem_prompt, mem_sample, pre_norm_g, w_in, gla_w_fwd, gla_b_fwd, gla_w_bwd,
           gla_b_bwd, gla_norm_g, nat_rpb, mem_norm_g, w_mem_kv, w_out, post_norm_g):
    assert pre_norm_g.shape[0] == 1, "single-layer trunk"
    w_in_p, wdec, bdec, w_kv, w_o = _prepare_weights(
        w_in[0], gla_w_fwd[0], gla_b_fwd[0], gla_w_bwd[0], gla_b_bwd[0], w_mem_kv[0], w_out[0])
    pre_g = pre_norm_g[0].reshape(1, D_MODEL)
    post_g = post_norm_g[0].reshape(1, D_MODEL)
    gla_ng = gla_norm_g[0].reshape(1, GLA_DV)
    mem_g = mem_norm_g[0].reshape(1, D_MODEL)
    bias_tab = _nat_bias_table(nat_rpb[0])

    n_p = mem_prompt.shape[0]
    mem_all = jnp.concatenate([mem_prompt, mem_sample], axis=0).reshape(-1, D_MODEL)
    mkv = _memkv(mem_all, mem_g, w_kv)
    mkv_p, mkv_s = mkv[:n_p * N_MEM], mkv[n_p * N_MEM:]

    run = functools.partial(_trunk, pre_g=pre_g, w_in_p=w_in_p, wdec=wdec, bdec=bdec, gla_ng=gla_ng,
                            bias_tab=bias_tab, w_out=w_o, post_g=post_g)
    return (run(x_prompt, mkv_p), run(x_sample, mkv_s))
```

```python
import functools

import numpy as np
import jax
import jax.numpy as jnp
from jax import lax
from jax.experimental import pallas as pl
from jax.experimental.pallas import tpu as pltpu

F32 = jnp.float32
BF16 = jnp.bfloat16

D_MODEL = 1024
N_MEM = 256
GRID_W = 64
EPS = 1e-6

GLA_HEADS = 4
GLA_DK = 128
GLA_DV = 256
GLA_RANK = 16
GLA_TAU = 16.0
GLA_QK = GLA_HEADS * GLA_DK
GLA_V = GLA_HEADS * GLA_DV

NAT_HEADS = 8
NAT_DH = 64
NAT_KH = 8
NAT_KW = 16
NAT_W = NAT_HEADS * NAT_DH
NAT_PAIRS = NAT_HEADS // 2

MEM_HEADS = 4
MEM_DH = 128
MEM_W = MEM_HEADS * MEM_DH

MIX_W = GLA_V + NAT_W + MEM_W

COL_BLK = 512
P_GQ, P_GK, P_GV, P_GG = 0, 512, 1024, 2048
P_NQ, P_NK, P_NV, P_NG = 3072, 3584, 4096, 4608
P_MQ, P_MG = 5120, 5632
P_LR = 6144
LR_PAD = 128
P_W = P_LR + LR_PAD

NEG_BIG = -1e30

VMEM_LIMIT = 56 * 1024 * 1024


def _cparams(n_axes):
    return pltpu.CompilerParams(
        dimension_semantics=("arbitrary",) * n_axes,
        vmem_limit_bytes=VMEM_LIMIT,
    )


def _resident(shape, index_map):
    return pl.BlockSpec(shape, index_map, pipeline_mode=pl.Buffered(1))


def _silu(g):
    return g * jax.nn.sigmoid(g)


def _inproj_kernel(x_ref, g_ref, w_ref, o_ref, *, scales):
    x = x_ref[...]
    ms = jnp.mean(x * x, axis=-1, keepdims=True)
    h = ((x * lax.rsqrt(ms + EPS)) * g_ref[...]).astype(BF16)
    for c0, c1, scale in scales:
        acc = jnp.dot(h, w_ref[:, c0:c1], preferred_element_type=F32)
        if scale is not None:
            acc = acc * scale
        o_ref[:, c0:c1] = acc.astype(BF16)


def _inproj(x2, pre_g, w_in_p, tm):
    rows = x2.shape[0]
    scales = []
    for c0 in range(0, P_LR, COL_BLK):
        scale = None
        if c0 == P_GQ:
            scale = GLA_DK ** -0.5
        elif c0 == P_NQ:
            scale = NAT_DH ** -0.5
        elif c0 == P_MQ:
            scale = MEM_DH ** -0.5
        scales.append((c0, c0 + COL_BLK, scale))
    scales.append((P_LR, P_W, None))
    return pl.pallas_call(
        functools.partial(_inproj_kernel, scales=tuple(scales)),
        grid=(rows // tm,),
        in_specs=[
            pl.BlockSpec((tm, D_MODEL), lambda i: (i, 0)),
            _resident((1, D_MODEL), lambda i: (0, 0)),
            _resident((D_MODEL, P_W), lambda i: (0, 0)),
        ],
        out_specs=pl.BlockSpec((tm, P_W), lambda i: (i, 0)),
        out_shape=jax.ShapeDtypeStruct((rows, P_W), BF16),
        compiler_params=_cparams(1),
        name="inproj",
    )(x2, pre_g, w_in_p)


def _block_row_bcast(b, blk, ridx):
    t = b.shape[0]
    if blk >= 8:
        pieces = []
        for b0 in range(0, t, blk):
            row = b[b0 + ridx:b0 + ridx + 1, :]
            pieces.append(jnp.broadcast_to(row, (blk, b.shape[1])))
        return pieces[0] if len(pieces) == 1 else jnp.concatenate(pieces, axis=0)
    m = lax.broadcasted_iota(jnp.int32, b.shape, 0) & (blk - 1)
    out = b
    for pos in range(blk):
        if pos == ridx:
            continue
        shifted = pltpu.roll(b, (pos - ridx) % t, axis=0)
        out = jnp.where(m == pos, shifted, out)
    return out


def _log_sigmoid(z):
    return jnp.minimum(z, 0.0) - jnp.log(1.0 + jnp.exp(-jnp.abs(z)))


LOG2_E = 1.4426950408889634


def _gla_prepare(q_ref, k_ref, lr_ref, wdec_ref, bdec_ref, *, reverse, t):
    d = 1 if reverse else 0
    q_all = q_ref[...]
    k_all = k_ref[...]
    z = jnp.dot(lr_ref[...], wdec_ref[:, d * GLA_QK:(d + 1) * GLA_QK], preferred_element_type=F32)
    z = z + bdec_ref[:, d * GLA_QK:(d + 1) * GLA_QK]
    la = _log_sigmoid(z) * (LOG2_E / GLA_TAU)

    ri = lax.broadcasted_iota(jnp.int32, (t, t), 0)
    ci = lax.broadcasted_iota(jnp.int32, (t, t), 1)
    tri = jnp.where((ci >= ri) if reverse else (ci <= ri), 1.0, 0.0).astype(BF16)
    la_hi = la.astype(BF16)
    la_lo = (la - la_hi.astype(F32)).astype(BF16)
    b = (jnp.dot(tri, la_hi, preferred_element_type=F32)
         + jnp.dot(tri, la_lo, preferred_element_type=F32))
    b_last = b[0:1, :] if reverse else b[t - 1:t, :]

    levels = []
    s = t // 2
    while s >= 1:
        ridx = s if reverse else s - 1
        e_l = jnp.exp2(-jnp.abs(b - _block_row_bcast(b, 2 * s, ridx))).astype(BF16)
        levels.append((s.bit_length() - 1, q_all * e_l, k_all * e_l))
        s //= 2
    return dict(
        q=q_all, k=k_all, levels=levels,
        q_in=q_all * jnp.exp2(b).astype(BF16),
        k_out=k_all * jnp.exp2(b_last - b).astype(BF16),
        e_all=jnp.exp2(b_last))


def _gla_head(pre, v_ref, o_ref, s_ref, h, *, reverse, t):
    half = t // 2
    rh = lax.broadcasted_iota(jnp.int32, (half, half), 0)
    ch = lax.broadcasted_iota(jnp.int32, (half, half), 1)
    xor_f = (rh ^ ch).astype(F32)
    level_of = lax.shift_right_logical(pltpu.bitcast(xor_f, jnp.int32), 23) - 127
    level_of = jnp.where((ch > rh) if reverse else (ch < rh), level_of, -1)

    nt_dims = (((1,), (1,)), ((), ()))
    early = slice(half, t) if reverse else slice(0, half)
    late = slice(0, half) if reverse else slice(half, t)
    kl = slice(h * GLA_DK, (h + 1) * GLA_DK)
    vl = slice(h * GLA_DV, (h + 1) * GLA_DV)
    vh = v_ref[:, vl]
    levels = pre["levels"]

    within = {}
    for rows in (early, late):
        qk = pre["q"][rows, kl].astype(F32) * pre["k"][rows, kl].astype(F32)
        att = jnp.where(rh == ch, jnp.sum(qk, axis=-1, keepdims=True), 0.0)
        for lvl, qt, kt in levels[1:]:
            p = lax.dot_general(qt[rows, kl], kt[rows, kl], nt_dims, preferred_element_type=F32)
            att = jnp.where(level_of == lvl, p, att)
        within[rows.start] = att.astype(BF16)
    _, qt, kt = levels[0]
    cross = lax.dot_general(qt[late, kl], kt[early, kl], nt_dims, preferred_element_type=F32).astype(BF16)
    blocks = [within[late.start], cross] if reverse else [cross, within[late.start]]
    o_late = jnp.dot(jnp.concatenate(blocks, axis=1), vh, preferred_element_type=F32)
    o_early = jnp.dot(within[early.start], vh[early], preferred_element_type=F32)

    st = s_ref[h]
    st_b = st.astype(BF16)
    for rows, o in ((early, o_early), (late, o_late)):
        o = o + lax.dot_general(pre["q_in"][rows, kl], st_b, nt_dims, preferred_element_type=F32)
        o_ref[rows, vl] = o.astype(o_ref.dtype)

    upd = lax.dot_general(vh, pre["k_out"][:, kl], (((0,), (0,)), ((), ())), preferred_element_type=F32)
    s_ref[h] = st * pre["e_all"][:, kl] + upd


def _gla_kernel(qf, kf, vf, lf, qb, kb, vb, lb, wdec, bdec, of, ob, sf, sb, *, t):
    @pl.when(pl.program_id(1) == 0)
    def _():
        sf[...] = jnp.zeros_like(sf)
        sb[...] = jnp.zeros_like(sb)

    pre_f = _gla_prepare(qf, kf, lf, wdec, bdec, reverse=False, t=t)
    pre_b = _gla_prepare(qb, kb, lb, wdec, bdec, reverse=True, t=t)
    for h in range(GLA_HEADS):
        _gla_head(pre_f, vf, of, sf, h, reverse=False, t=t)
        _gla_head(pre_b, vb, ob, sb, h, reverse=True, t=t)


def _gla(p, wdec, bdec, n_seq, n_tok, t):
    nt = n_tok // t
    rows = n_seq * n_tok

    def fwd(col):
        return lambda s, i: (s * nt + i, col)

    def bwd(col):
        return lambda s, i: (s * nt + nt - 1 - i, col)

    def specs(m):
        return [
            pl.BlockSpec((t, GLA_QK), m(P_GQ // GLA_QK)),
            pl.BlockSpec((t, GLA_QK), m(P_GK // GLA_QK)),
            pl.BlockSpec((t, GLA_V), m(P_GV // GLA_V)),
            pl.BlockSpec((t, LR_PAD), m(P_LR // LR_PAD)),
        ]

    out_sds = jax.ShapeDtypeStruct((rows, GLA_V), BF16)
    return pl.pallas_call(
        functools.partial(_gla_kernel, t=t),
        grid=(n_seq, nt),
        in_specs=specs(fwd) + specs(bwd) + [
            _resident((LR_PAD, 2 * GLA_QK), lambda s, i: (0, 0)),
            _resident((1, 2 * GLA_QK), lambda s, i: (0, 0)),
        ],
        out_specs=[pl.BlockSpec((t, GLA_V), fwd(0)), pl.BlockSpec((t, GLA_V), bwd(0))],
        out_shape=[out_sds, out_sds],
        scratch_shapes=[pltpu.VMEM((GLA_HEADS, GLA_DV, GLA_DK), F32),
                        pltpu.VMEM((GLA_HEADS, GLA_DV, GLA_DK), F32)],
        compiler_params=_cparams(2),
        name="gla",
    )(p, p, p, p, p, p, p, p, wdec, bdec)


NAT_TILE_ROWS = 8
NAT_TILE = NAT_TILE_ROWS * GRID_W
NAT_KEYS = NAT_KH * GRID_W


NAT_BIAS_ROWS = 2 * NAT_KH - 2
PAIR_W = 2 * NAT_DH


def _nat_bias_table(rpb):
    cq = np.arange(GRID_W)[:, None]
    ck = np.arange(GRID_W)[None, :]
    cs = np.clip(cq - NAT_KW // 2, 0, GRID_W - NAT_KW)
    col_in = (ck >= cs) & (ck < cs + NAT_KW)
    dcol = np.clip(ck - cq + (NAT_KW - 1), 0, 2 * NAT_KW - 2)
    onehot = ((dcol[None] == np.arange(2 * NAT_KW - 1)[:, None, None]) & col_in[None]).astype(np.float32)
    tab = jnp.einsum("hrd,dqk->hrqk", rpb.astype(F32), jnp.asarray(onehot),
                     precision=lax.Precision.HIGHEST)
    tab = tab + jnp.asarray(np.where(col_in, 0.0, NEG_BIG).astype(np.float32))
    tab = jnp.concatenate([tab[:, :-1], tab[:, 1:]], axis=-1)
    tab = tab.reshape(NAT_PAIRS, 2, NAT_BIAS_ROWS, GRID_W, PAIR_W)
    return tab.transpose(0, 2, 1, 3, 4).reshape(NAT_PAIRS, NAT_BIAS_ROWS, PAIR_W, PAIR_W)


def _nat_kernel(q_ref, kp_ref, kc_ref, kn_ref, vp_ref, vc_ref, vn_ref, g_ref, bias_ref, o_ref,
                kbuf, vbuf, *, n_rows):
    i = pl.program_id(1)
    kbuf[0:NAT_TILE, :] = kp_ref[...]
    kbuf[NAT_TILE:2 * NAT_TILE, :] = kc_ref[...]
    kbuf[2 * NAT_TILE:3 * NAT_TILE, :] = kn_ref[...]
    ones = jnp.ones((NAT_TILE, PAIR_W), BF16)
    for j, v_ref in enumerate((vp_ref, vc_ref, vn_ref)):
        for p in range(NAT_PAIRS):
            vbuf[j * NAT_TILE:(j + 1) * NAT_TILE, 2 * p * PAIR_W:(2 * p + 1) * PAIR_W] = (
                v_ref[:, p * PAIR_W:(p + 1) * PAIR_W])
            vbuf[j * NAT_TILE:(j + 1) * NAT_TILE, (2 * p + 1) * PAIR_W:(2 * p + 2) * PAIR_W] = ones

    lane = lax.broadcasted_iota(jnp.int32, (GRID_W, PAIR_W), 1)
    first = lane < NAT_DH
    r0 = i * NAT_TILE_ROWS
    for rr in range(NAT_TILE_ROWS):
        r = r0 + rr
        rs = jnp.clip(r - NAT_KH // 2, 0, n_rows - NAT_KH)
        d0 = rs - r + (NAT_KH - 1)
        off = pl.multiple_of((rs - r0 + NAT_TILE_ROWS) * GRID_W, GRID_W)
        rows = slice(rr * GRID_W, (rr + 1) * GRID_W)
        scores = []
        for p in range(NAT_PAIRS):
            lanes = slice(p * PAIR_W, (p + 1) * PAIR_W)
            qp = q_ref[rows, lanes]
            zero = jnp.zeros_like(qp)
            q2 = jnp.concatenate([jnp.where(first, qp, zero), jnp.where(first, zero, qp)], axis=0)
            kk = kbuf[pl.ds(off, NAT_KEYS), lanes]
            s = lax.dot_general(q2, kk, (((1,), (1,)), ((), ())), preferred_element_type=F32)
            bias = jnp.concatenate([bias_ref[p, d0 + 2 * j] for j in range(NAT_KH // 2)], axis=1)
            scores.append(s + bias)
        probs = []
        for s in scores:
            probs.append(jnp.exp(s - jnp.max(s, axis=-1, keepdims=True)).astype(BF16))
        for p, e in enumerate(probs):
            lanes = slice(p * PAIR_W, (p + 1) * PAIR_W)
            vv = vbuf[pl.ds(off, NAT_KEYS), 2 * p * PAIR_W:(2 * p + 2) * PAIR_W]
            pv = jnp.dot(e, vv, preferred_element_type=F32)
            pv = pv[:, 0:PAIR_W] / pv[:, PAIR_W:2 * PAIR_W]
            o = jnp.where(first, pv[0:GRID_W, :], pv[GRID_W:2 * GRID_W, :])
            g = g_ref[rows, lanes].astype(F32)
            o_ref[rows, lanes] = (o * _silu(g)).astype(o_ref.dtype)


def _nat(p, bias_tab, n_seq, n_tok):
    nt = n_tok // NAT_TILE
    n_rows = n_tok // GRID_W
    rows = n_seq * n_tok
    qcol, kcol, vcol, gcol = (c // NAT_W for c in (P_NQ, P_NK, P_NV, P_NG))

    def at(col, shift):
        def index(s, i):
            return (s * nt + jnp.clip(i + shift, 0, nt - 1), col)
        return index

    blk = (NAT_TILE, NAT_W)
    return pl.pallas_call(
        functools.partial(_nat_kernel, n_rows=n_rows),
        grid=(n_seq, nt),
        in_specs=[
            pl.BlockSpec(blk, at(qcol, 0)),
            pl.BlockSpec(blk, at(kcol, -1)), pl.BlockSpec(blk, at(kcol, 0)), pl.BlockSpec(blk, at(kcol, 1)),
            pl.BlockSpec(blk, at(vcol, -1)), pl.BlockSpec(blk, at(vcol, 0)), pl.BlockSpec(blk, at(vcol, 1)),
            pl.BlockSpec(blk, at(gcol, 0)),
            _resident(bias_tab.shape, lambda s, i: (0, 0, 0, 0)),
        ],
        out_specs=pl.BlockSpec(blk, at(0, 0)),
        out_shape=jax.ShapeDtypeStruct((rows, NAT_W), BF16),
        scratch_shapes=[pltpu.VMEM((3 * NAT_TILE, NAT_W), BF16), pltpu.VMEM((3 * NAT_TILE, 2 * NAT_W), BF16)],
        compiler_params=_cparams(2),
        name="nat",
    )(p, p, p, p, p, p, p, p, bias_tab)


MKV_W = MEM_W + 2 * MEM_W


def _memkv_kernel(m_ref, g_ref, w_ref, o_ref):
    x = m_ref[...]
    ms = jnp.mean(x * x, axis=-1, keepdims=True)
    h = ((x * lax.rsqrt(ms + EPS)) * g_ref[...]).astype(BF16)
    kv = jnp.dot(h, w_ref[...], preferred_element_type=F32).astype(o_ref.dtype)
    o_ref[:, 0:MEM_W] = kv[:, 0:MEM_W]
    ones = jnp.ones((N_MEM, MEM_DH), o_ref.dtype)
    for hd in range(MEM_HEADS):
        c0 = MEM_W + 2 * hd * MEM_DH
        o_ref[:, c0:c0 + MEM_DH] = kv[:, MEM_W + hd * MEM_DH:MEM_W + (hd + 1) * MEM_DH]
        o_ref[:, c0 + MEM_DH:c0 + 2 * MEM_DH] = ones


def _memkv(mem2, mem_g, w_kv):
    rows = mem2.shape[0]
    return pl.pallas_call(
        _memkv_kernel,
        grid=(rows // N_MEM,),
        in_specs=[
            pl.BlockSpec((N_MEM, D_MODEL), lambda i: (i, 0)),
            _resident((1, D_MODEL), lambda i: (0, 0)),
            _resident((D_MODEL, 2 * MEM_W), lambda i: (0, 0)),
        ],
        out_specs=pl.BlockSpec((N_MEM, MKV_W), lambda i: (i, 0)),
        out_shape=jax.ShapeDtypeStruct((rows, MKV_W), BF16),
        compiler_params=_cparams(1),
        name="memkv",
    )(mem2, mem_g, w_kv)


def _memattn_kernel(q_ref, g_ref, kv_ref, o_ref):
    for h in range(MEM_HEADS):
        lanes = slice(h * MEM_DH, (h + 1) * MEM_DH)
        kh = kv_ref[:, lanes]
        vh = kv_ref[:, MEM_W + 2 * h * MEM_DH:MEM_W + 2 * (h + 1) * MEM_DH]
        s = lax.dot_general(q_ref[:, lanes], kh, (((1,), (1,)), ((), ())), preferred_element_type=F32)
        e = jnp.exp(s - jnp.max(s, axis=-1, keepdims=True)).astype(BF16)
        pv = jnp.dot(e, vh, preferred_element_type=F32)
        o = pv[:, 0:MEM_DH] / pv[:, MEM_DH:2 * MEM_DH]
        g = g_ref[:, lanes].astype(F32)
        o_ref[:, lanes] = (o * _silu(g)).astype(o_ref.dtype)


def _memattn(p, mkv, n_seq, n_tok, tq):
    nt = n_tok // tq
    rows = n_seq * n_tok
    return pl.pallas_call(
        _memattn_kernel,
        grid=(n_seq, nt),
        in_specs=[
            pl.BlockSpec((tq, MEM_W), lambda s, i: (s * nt + i, P_MQ // MEM_W)),
            pl.BlockSpec((tq, MEM_W), lambda s, i: (s * nt + i, P_MG // MEM_W)),
            pl.BlockSpec((N_MEM, MKV_W), lambda s, i: (s, 0)),
        ],
        out_specs=pl.BlockSpec((tq, MEM_W), lambda s, i: (s * nt + i, 0)),
        out_shape=jax.ShapeDtypeStruct((rows, MEM_W), BF16),
        compiler_params=_cparams(2),
        name="memattn",
    )(p, p, mkv)


def _out_kernel(of_ref, ob_ref, gg_ref, nat_ref, mem_ref, x_ref, w_ref, gng_ref, post_ref, y_ref):
    acc = jnp.dot(nat_ref[...], w_ref[GLA_V:GLA_V + NAT_W, :], preferred_element_type=F32)
    acc = acc + jnp.dot(mem_ref[...], w_ref[GLA_V + NAT_W:MIX_W, :], preferred_element_type=F32)
    for h in range(GLA_HEADS):
        lanes = slice(h * GLA_DV, (h + 1) * GLA_DV)
        o = of_ref[:, lanes].astype(F32) + ob_ref[:, lanes].astype(F32)
        ms = jnp.mean(o * o, axis=-1, keepdims=True)
        on = (o * lax.rsqrt(ms + EPS)) * gng_ref[...]
        og = (on * _silu(gg_ref[:, lanes].astype(F32))).astype(BF16)
        acc = acc + jnp.dot(og, w_ref[lanes, :], preferred_element_type=F32)
    ms = jnp.mean(acc * acc, axis=-1, keepdims=True)
    y_ref[...] = x_ref[...] + (acc * lax.rsqrt(ms + EPS)) * post_ref[...]


def _out(o_f, o_b, p, o_nat, o_mem, x2, w_out, gla_ng, post_g, tm):
    rows = x2.shape[0]
    return pl.pallas_call(
        _out_kernel,
        grid=(rows // tm,),
        in_specs=[
            pl.BlockSpec((tm, GLA_V), lambda i: (i, 0)),
            pl.BlockSpec((tm, GLA_V), lambda i: (i, 0)),
            pl.BlockSpec((tm, GLA_V), lambda i: (i, P_GG // GLA_V)),
            pl.BlockSpec((tm, NAT_W), lambda i: (i, 0)),
            pl.BlockSpec((tm, MEM_W), lambda i: (i, 0)),
            pl.BlockSpec((tm, D_MODEL), lambda i: (i, 0)),
            _resident((MIX_W, D_MODEL), lambda i: (0, 0)),
            _resident((1, GLA_DV), lambda i: (0, 0)),
            _resident((1, D_MODEL), lambda i: (0, 0)),
        ],
        out_specs=pl.BlockSpec((tm, D_MODEL), lambda i: (i, 0)),
        out_shape=jax.ShapeDtypeStruct((rows, D_MODEL), F32),
        compiler_params=_cparams(1),
        name="outproj",
    )(o_f, o_b, p, o_nat, o_mem, x2, w_out, gla_ng, post_g)


def _choose_tile(n, pref):
    t = pref
    while n % t:
        t //= 2
    return t


def _prepare_weights(w_in, gw_f, gb_f, gw_b, gb_b, w_mem_kv, w_out):
    offs = np.cumsum([0, GLA_QK, GLA_QK, GLA_V, GLA_V, GLA_RANK, GLA_RANK,
                      NAT_W, NAT_W, NAT_W, NAT_W, MEM_W, MEM_W])
    w16 = w_in.astype(BF16)
    seg = [w16[:, offs[k]:offs[k + 1]] for k in range(12)]
    gq, gk, gv, gg, lrf, lrb, nq, nk, nv, ng, mq, mg = seg
    pad = jnp.zeros((D_MODEL, LR_PAD - 2 * GLA_RANK), BF16)
    w_in_p = jnp.concatenate([gq, gk, gv, gg, nq, nk, nv, ng, mq, mg, lrf, lrb, pad], axis=1)
    wdec = jnp.zeros((LR_PAD, 2 * GLA_QK), F32)
    wdec = wdec.at[0:GLA_RANK, 0:GLA_QK].set(gw_f)
    wdec = wdec.at[GLA_RANK:2 * GLA_RANK, GLA_QK:].set(gw_b)
    bdec = jnp.concatenate([gb_f, gb_b]).reshape(1, 2 * GLA_QK).astype(F32)
    return w_in_p, wdec.astype(BF16), bdec, w_mem_kv.astype(BF16), w_out.astype(BF16)


def _trunk(x, mkv, pre_g, w_in_p, wdec, bdec, gla_ng, bias_tab, w_out, post_g):
    n_seq, n_tok, _ = x.shape
    rows = n_seq * n_tok
    x2 = x.reshape(rows, D_MODEL)
    tm = _choose_tile(rows, 512)
    p = _inproj(x2, pre_g, w_in_p, tm)
    o_f, o_b = _gla(p, wdec, bdec, n_seq, n_tok, _choose_tile(n_tok, 256))
    o_nat = _nat(p, bias_tab, n_seq, n_tok)
    o_mem = _memattn(p, mkv, n_seq, n_tok, _choose_tile(n_tok, 512))
    y = _out(o_f, o_b, p, o_nat, o_mem, x2, w_out, gla_ng, post_g, tm)
    return y.reshape(x.shape)


def kernel(x_prompt, x_sample, mem_prompt, mem_sample, pre_norm_g, w_in, gla_w_fwd, gla_b_fwd, gla_w_bwd,
           gla_b_bwd, gla_norm_g, nat_rpb, mem_norm_g, w_mem_kv, w_out, post_norm_g):
    assert pre_norm_g.shape[0] == 1, "single-layer trunk"
    w_in_p, wdec, bdec, w_kv, w_o = _prepare_weights(
        w_in[0], gla_w_fwd[0], gla_b_fwd[0], gla_w_bwd[0], gla_b_bwd[0], w_mem_kv[0], w_out[0])
    pre_g = pre_norm_g[0].reshape(1, D_MODEL)
    post_g = post_norm_g[0].reshape(1, D_MODEL)
    gla_ng = gla_norm_g[0].reshape(1, GLA_DV)
    mem_g = mem_norm_g[0].reshape(1, D_MODEL)
    bias_tab = _nat_bias_table(nat_rpb[0])

    n_p = mem_prompt.shape[0]
    mem_all = jnp.concatenate([mem_prompt, mem_sample], axis=0).reshape(-1, D_MODEL)
    mkv = _memkv(mem_all, mem_g, w_kv)
    mkv_p, mkv_s = mkv[:n_p * N_MEM], mkv[n_p * N_MEM:]

    run = functools.partial(_trunk, pre_g=pre_g, w_in_p=w_in_p, wdec=wdec, bdec=bdec, gla_ng=gla_ng,
                            bias_tab=bias_tab, w_out=w_o, post_g=post_g)
    return (run(x_prompt, mkv_p), run(x_sample, mkv_s))
```

```python
import functools

import numpy as np
import jax
import jax.numpy as jnp
from jax import lax
from jax.experimental import pallas as pl
from jax.experimental.pallas import tpu as pltpu

F32 = jnp.float32
BF16 = jnp.bfloat16

D_MODEL = 1024
N_MEM = 256
GRID_W = 64
EPS = 1e-6

GLA_HEADS = 4
GLA_DK = 128
GLA_DV = 256
GLA_RANK = 16
GLA_TAU = 16.0
GLA_QK = GLA_HEADS * GLA_DK
GLA_V = GLA_HEADS * GLA_DV

NAT_HEADS = 8
NAT_DH = 64
NAT_KH = 8
NAT_KW = 16
NAT_W = NAT_HEADS * NAT_DH
NAT_PAIRS = NAT_HEADS // 2

MEM_HEADS = 4
MEM_DH = 128
MEM_W = MEM_HEADS * MEM_DH

MIX_W = GLA_V + NAT_W + MEM_W

COL_BLK = 512
P_GQ, P_GK, P_GV, P_GG = 0, 512, 1024, 2048
P_NQ, P_NK, P_NV, P_NG = 3072, 3584, 4096, 4608
P_MQ, P_MG = 5120, 5632
P_LR = 6144
LR_PAD = 128
P_W = P_LR + LR_PAD

NEG_BIG = -1e30

VMEM_LIMIT = 56 * 1024 * 1024


def _cparams(n_axes):
    return pltpu.CompilerParams(
        dimension_semantics=("arbitrary",) * n_axes,
        vmem_limit_bytes=VMEM_LIMIT,
    )


def _resident(shape, index_map):
    return pl.BlockSpec(shape, index_map, pipeline_mode=pl.Buffered(1))


def _silu(g):
    return g * jax.nn.sigmoid(g)


def _normed_input(x_ref, g_ref):
    x = x_ref[...]
    ms = jnp.mean(x * x, axis=-1, keepdims=True)
    return ((x * lax.rsqrt(ms + EPS)) * g_ref[...]).astype(BF16)


def _gla_input_col(c0):
    if c0 < P_GV + GLA_V:
        return c0
    return 2 * GLA_QK + GLA_V if c0 == P_LR else None


def _project_chunks(h, w_ref, o_ref, chunks, gla_ref=None):
    for c0, c1, post in chunks:
        acc = jnp.dot(h, w_ref[:, c0:c1], preferred_element_type=F32)
        if post == "gate":
            acc = _silu(acc)
        elif post is not None:
            acc = acc * post
        val = acc.astype(BF16)
        o_ref[:, c0:c1] = val
        g0 = _gla_input_col(c0)
        if gla_ref is not None and g0 is not None:
            gla_ref[:, g0:g0 + (c1 - c0)] = val


def _proj_scales():
    post = {P_GQ: GLA_DK ** -0.5, P_NQ: NAT_DH ** -0.5, P_MQ: MEM_DH ** -0.5,
            P_GG: "gate", P_GG + COL_BLK: "gate", P_NG: "gate", P_MG: "gate"}
    chunks = [(c0, c0 + COL_BLK, post.get(c0)) for c0 in range(0, P_LR, COL_BLK)]
    return tuple(chunks + [(P_LR, P_W, None)])


def _block_row_bcast(b, blk, ridx):
    t = b.shape[0]
    if blk >= 8:
        pieces = []
        for b0 in range(0, t, blk):
            row = b[b0 + ridx:b0 + ridx + 1, :]
            pieces.append(jnp.broadcast_to(row, (blk, b.shape[1])))
        return pieces[0] if len(pieces) == 1 else jnp.concatenate(pieces, axis=0)
    m = lax.broadcasted_iota(jnp.int32, b.shape, 0) & (blk - 1)
    out = b
    for pos in range(blk):
        if pos == ridx:
            continue
        shifted = pltpu.roll(b, (pos - ridx) % t, axis=0)
        out = jnp.where(m == pos, shifted, out)
    return out


class _Window:
    def __init__(self, ref, rows, cols):
        self.ref, self.rows, self.cols, self.dtype = ref, rows, cols, ref.dtype

    def _abs(self, idx):
        idx = (slice(None), slice(None)) if idx is Ellipsis else idx
        out = []
        for sl, base in zip(idx, (self.rows, self.cols)):
            lo = base.start + (sl.start or 0)
            hi = base.stop if sl.stop is None else base.start + sl.stop
            out.append(slice(lo, hi))
        return tuple(out)

    def __getitem__(self, idx):
        return self.ref[self._abs(idx)]

    def __setitem__(self, idx, value):
        self.ref[self._abs(idx)] = value


def _log_sigmoid(z):
    return jnp.minimum(z, 0.0) - jnp.log(1.0 + jnp.exp(-jnp.abs(z)))


LOG2_E = 1.4426950408889634


def _gla_decay(lr_ref, wdec_ref, bdec_ref, *, reverse, t):
    d = 1 if reverse else 0
    z = jnp.dot(lr_ref[...], wdec_ref[:, d * GLA_QK:(d + 1) * GLA_QK], preferred_element_type=F32)
    z = z + bdec_ref[:, d * GLA_QK:(d + 1) * GLA_QK]
    la = _log_sigmoid(z) * (LOG2_E / GLA_TAU)

    ri = lax.broadcasted_iota(jnp.int32, (t, t), 0)
    ci = lax.broadcasted_iota(jnp.int32, (t, t), 1)
    tri = jnp.where((ci >= ri) if reverse else (ci <= ri), 1.0, 0.0).astype(BF16)
    la_hi = la.astype(BF16)
    la_lo = (la - la_hi.astype(F32)).astype(BF16)
    b = (jnp.dot(tri, la_hi, preferred_element_type=F32)
         + jnp.dot(tri, la_lo, preferred_element_type=F32))
    return b


def _gla_role_signs(t, reverse):
    rows = np.arange(t)
    signs = []
    s = t // 2
    while s >= 1:
        later = (rows & (2 * s - 1)) >= s
        signs.append(np.where(later != reverse, 1.0, -1.0))
        s //= 2
    return np.ascontiguousarray(np.broadcast_to(np.stack(signs)[:, :, None], (len(signs), t, GLA_DK)),
                                dtype=np.float32)


def _gla_factors(b, q_ref, k_ref, sign_ref, *, reverse, t):
    q_all = q_ref[...]
    k_all = k_ref[...]
    b_last = b[0:1, :] if reverse else b[t - 1:t, :]

    levels = []
    s = t // 2
    while s >= 1:
        ridx = s if reverse else s - 1
        dist = b - _block_row_bcast(b, 2 * s, ridx)
        sign = sign_ref[len(levels)]
        neg = jnp.concatenate([dist[:, h * GLA_DK:(h + 1) * GLA_DK] * sign for h in range(GLA_HEADS)], axis=1)
        e_l = jnp.exp2(neg).astype(BF16)
        levels.append((s.bit_length() - 1, q_all * e_l, k_all * e_l))
        s //= 2
    return dict(
        q=q_all, k=k_all, levels=levels,
        q_in=q_all * jnp.exp2(b).astype(BF16),
        k_out=k_all * jnp.exp2(b_last - b).astype(BF16),
        e_all=jnp.exp2(b_last))


def _gla_head(pre, v_ref, o_ref, s_ref, h, *, reverse, t):
    half = t // 2
    rh = lax.broadcasted_iota(jnp.int32, (half, half), 0)
    ch = lax.broadcasted_iota(jnp.int32, (half, half), 1)
    xor_f = (rh ^ ch).astype(F32)
    level_of = lax.shift_right_logical(pltpu.bitcast(xor_f, jnp.int32), 23) - 127
    level_of = jnp.where((ch > rh) if reverse else (ch < rh), level_of, -1)

    nt_dims = (((1,), (1,)), ((), ()))
    early = slice(half, t) if reverse else slice(0, half)
    late = slice(0, half) if reverse else slice(half, t)
    kl = slice(h * GLA_DK, (h + 1) * GLA_DK)
    vl = slice(h * GLA_DV, (h + 1) * GLA_DV)
    vh = v_ref[:, vl]
    levels = pre["levels"]

    within = {}
    for rows in (early, late):
        qk = pre["q"][rows, kl].astype(F32) * pre["k"][rows, kl].astype(F32)
        att = jnp.where(rh == ch, jnp.sum(qk, axis=-1, keepdims=True), 0.0)
        for lvl, qt, kt in levels[1:]:
            p = lax.dot_general(qt[rows, kl], kt[rows, kl], nt_dims, preferred_element_type=F32)
            att = jnp.where(level_of == lvl, p, att)
        within[rows.start] = att.astype(BF16)
    _, qt, kt = levels[0]
    cross = lax.dot_general(qt[late, kl], kt[early, kl], nt_dims, preferred_element_type=F32).astype(BF16)
    blocks = [within[late.start], cross] if reverse else [cross, within[late.start]]
    o_late = jnp.dot(jnp.concatenate(blocks, axis=1), vh, preferred_element_type=F32)
    o_early = jnp.dot(within[early.start], vh[early], preferred_element_type=F32)

    st = s_ref[h]
    st_b = st.astype(BF16)
    for rows, o in ((early, o_early), (late, o_late)):
        o = o + lax.dot_general(pre["q_in"][rows, kl], st_b, nt_dims, preferred_element_type=F32)
        o_ref[rows, vl] = o.astype(o_ref.dtype)

    upd = lax.dot_general(vh, pre["k_out"][:, kl], (((0,), (0,)), ((), ())), preferred_element_type=F32)
    s_ref[h] = st * pre["e_all"][:, kl] + upd


GLA_IN_W = 2 * GLA_QK + GLA_V + LR_PAD


def _proj_gla_bwd_kernel(x_ref, g_ref, w_ref, wdec, bdec, sign_ref, p_ref, ob_ref, s_ref, even_ref, odd_ref, *,
                         scales, t, nc):
    step = pl.program_id(1)

    @pl.when(step == 0)
    def _():
        s_ref[...] = jnp.zeros_like(s_ref)
        odd_ref[...] = jnp.zeros_like(odd_ref)

    body = functools.partial(_proj_gla_bwd_step, x_ref, g_ref, w_ref, wdec, bdec, sign_ref, p_ref, ob_ref, s_ref,
                             scales=scales, t=t, nc=nc)
    pl.when(step % 2 == 0)(lambda: body(odd_ref, even_ref))
    pl.when(step % 2 == 1)(lambda: body(even_ref, odd_ref))


def _proj_gla_bwd_step(x_ref, g_ref, w_ref, wdec, bdec, sign_ref, p_ref, ob_ref, s_ref, pend_ref, next_ref, *,
                       scales, t, nc):
    order = list(reversed(range(nc)))
    rows_of = {c: slice(c * t, (c + 1) * t) for c in order}

    def pending(c, c0, c1):
        return _Window(pend_ref, rows_of[c], slice(c0, c1))

    b = {c: _gla_decay(pending(c, 2 * GLA_QK + GLA_V, GLA_IN_W), wdec, bdec, reverse=True, t=t)
         for c in order}
    hx = _normed_input(x_ref, g_ref)
    n_slots = nc * GLA_HEADS
    per_slot = max(1, (len(scales) - 4) // n_slots)
    n_first = len(scales) - per_slot * n_slots
    _project_chunks(hx, w_ref, p_ref, scales[:n_first], next_ref)
    pre = {c: _gla_factors(b[c], pending(c, 0, GLA_QK), pending(c, GLA_QK, 2 * GLA_QK), sign_ref,
                           reverse=True, t=t) for c in order}
    slot = 0
    for c in order:
        v_ref = pending(c, 2 * GLA_QK, 2 * GLA_QK + GLA_V)
        o_win = _Window(ob_ref, rows_of[c], slice(0, GLA_V))
        for h in range(GLA_HEADS):
            _project_chunks(hx, w_ref, p_ref, scales[n_first + slot * per_slot:n_first + (slot + 1) * per_slot],
                            next_ref)
            slot += 1
            _gla_head(pre[c], v_ref, o_win, s_ref, h, reverse=True, t=t)


def _proj_gla_bwd(x2, pre_g, w_in_p, wdec, bdec, n_seq, n_tok, t, nc):
    tb = t * nc
    nt = n_tok // tb
    rows = n_seq * n_tok
    assert (P_GQ, P_GK, P_GV) == (0, GLA_QK, 2 * GLA_QK)
    signs = jnp.asarray(_gla_role_signs(t, reverse=True))

    def proj_blk(s, i):
        return (s * nt + nt - 1 - jnp.minimum(i, nt - 1), 0)

    def gla_blk(s, i):
        return (s * nt + nt - 1 - jnp.maximum(i - 1, 0), 0)

    return pl.pallas_call(
        functools.partial(_proj_gla_bwd_kernel, scales=_proj_scales(), t=t, nc=nc),
        grid=(n_seq, nt + 1),
        in_specs=[
            pl.BlockSpec((tb, D_MODEL), proj_blk),
            _resident((1, D_MODEL), lambda s, i: (0, 0)),
            _resident((D_MODEL, P_W), lambda s, i: (0, 0)),
            _resident((LR_PAD, 2 * GLA_QK), lambda s, i: (0, 0)),
            _resident((1, 2 * GLA_QK), lambda s, i: (0, 0)),
            _resident(signs.shape, lambda s, i: (0, 0, 0)),
        ],
        out_specs=[pl.BlockSpec((tb, P_W), proj_blk), pl.BlockSpec((tb, GLA_V), gla_blk)],
        out_shape=[jax.ShapeDtypeStruct((rows, P_W), BF16), jax.ShapeDtypeStruct((rows, GLA_V), BF16)],
        scratch_shapes=[pltpu.VMEM((GLA_HEADS, GLA_DV, GLA_DK), F32),
                        pltpu.VMEM((tb, GLA_IN_W), BF16), pltpu.VMEM((tb, GLA_IN_W), BF16)],
        compiler_params=_cparams(2),
        name="proj_gla_bwd",
    )(x2, pre_g, w_in_p, wdec, bdec, signs)


def _gla_fwd_out_kernel(q_ref, k_ref, v_ref, lr_ref, wdec, bdec, sign_ref, ob_ref, gg_ref, nat_ref, mem_ref, x_ref,
                        w_ref, gng_ref, post_ref, y_ref, s_ref, even_ref, odd_ref, *, t, nc):
    step = pl.program_id(1)

    @pl.when(step == 0)
    def _():
        s_ref[...] = jnp.zeros_like(s_ref)
        odd_ref[...] = jnp.zeros_like(odd_ref)

    body = functools.partial(_gla_fwd_out_step, q_ref, k_ref, v_ref, lr_ref, wdec, bdec, sign_ref, ob_ref, gg_ref,
                             nat_ref, mem_ref, x_ref, w_ref, gng_ref, post_ref, y_ref, s_ref, t=t, nc=nc)
    pl.when(step % 2 == 0)(lambda: body(odd_ref, even_ref))
    pl.when(step % 2 == 1)(lambda: body(even_ref, odd_ref))


def _gla_fwd_out_step(q_ref, k_ref, v_ref, lr_ref, wdec, bdec, sign_ref, ob_ref, gg_ref, nat_ref, mem_ref, x_ref,
                      w_ref, gng_ref, post_ref, y_ref, s_ref, of_pend, of_new, *, t, nc):
    chunk_rows = [slice(c * t, (c + 1) * t) for c in range(nc)]
    b = [_gla_decay(_Window(lr_ref, r, slice(0, LR_PAD)), wdec, bdec, reverse=False, t=t) for r in chunk_rows]

    acc = jnp.dot(nat_ref[...], w_ref[GLA_V:GLA_V + NAT_W, :], preferred_element_type=F32)
    acc = acc + jnp.dot(mem_ref[...], w_ref[GLA_V + NAT_W:MIX_W, :], preferred_element_type=F32)
    for h in range(GLA_HEADS):
        lanes = slice(h * GLA_DV, (h + 1) * GLA_DV)
        o = of_pend[:, lanes].astype(F32) + ob_ref[:, lanes].astype(F32)
        ms = jnp.mean(o * o, axis=-1, keepdims=True)
        on = (o * lax.rsqrt(ms + EPS)) * gng_ref[...]
        og = (on * gg_ref[:, lanes].astype(F32)).astype(BF16)
        acc = acc + jnp.dot(og, w_ref[lanes, :], preferred_element_type=F32)
    ms = jnp.mean(acc * acc, axis=-1, keepdims=True)
    y_ref[...] = x_ref[...] + (acc * lax.rsqrt(ms + EPS)) * post_ref[...]

    pre = [_gla_factors(b[c], _Window(q_ref, r, slice(0, GLA_QK)), _Window(k_ref, r, slice(0, GLA_QK)), sign_ref,
                        reverse=False, t=t)
           for c, r in enumerate(chunk_rows)]
    for c, r in enumerate(chunk_rows):
        for h in range(GLA_HEADS):
            _gla_head(pre[c], _Window(v_ref, r, slice(0, GLA_V)), _Window(of_new, r, slice(0, GLA_V)), s_ref, h,
                      reverse=False, t=t)


def _gla_fwd_out(p, o_b, o_nat, o_mem, x2, wdec, bdec, w_out, gla_ng, post_g, n_seq, n_tok, t, nc):
    tb = t * nc
    nt = n_tok // tb
    rows = n_seq * n_tok

    signs = jnp.asarray(_gla_role_signs(t, reverse=False))

    def gla_blk(col):
        return lambda s, i: (s * nt + jnp.minimum(i, nt - 1), col)

    def out_blk(col):
        return lambda s, i: (s * nt + jnp.maximum(i - 1, 0), col)

    return pl.pallas_call(
        functools.partial(_gla_fwd_out_kernel, t=t, nc=nc),
        grid=(n_seq, nt + 1),
        in_specs=[
            pl.BlockSpec((tb, GLA_QK), gla_blk(P_GQ // GLA_QK)),
            pl.BlockSpec((tb, GLA_QK), gla_blk(P_GK // GLA_QK)),
            pl.BlockSpec((tb, GLA_V), gla_blk(P_GV // GLA_V)),
            pl.BlockSpec((tb, LR_PAD), gla_blk(P_LR // LR_PAD)),
            _resident((LR_PAD, 2 * GLA_QK), lambda s, i: (0, 0)),
            _resident((1, 2 * GLA_QK), lambda s, i: (0, 0)),
            _resident(signs.shape, lambda s, i: (0, 0, 0)),
            pl.BlockSpec((tb, GLA_V), out_blk(0)),
            pl.BlockSpec((tb, GLA_V), out_blk(P_GG // GLA_V)),
            pl.BlockSpec((tb, NAT_W), out_blk(0)),
            pl.BlockSpec((tb, MEM_W), out_blk(0)),
            pl.BlockSpec((tb, D_MODEL), out_blk(0)),
            _resident((MIX_W, D_MODEL), lambda s, i: (0, 0)),
            _resident((1, GLA_DV), lambda s, i: (0, 0)),
            _resident((1, D_MODEL), lambda s, i: (0, 0)),
        ],
        out_specs=pl.BlockSpec((tb, D_MODEL), out_blk(0)),
        out_shape=jax.ShapeDtypeStruct((rows, D_MODEL), F32),
        scratch_shapes=[pltpu.VMEM((GLA_HEADS, GLA_DV, GLA_DK), F32),
                        pltpu.VMEM((tb, GLA_V), BF16), pltpu.VMEM((tb, GLA_V), BF16)],
        compiler_params=_cparams(2),
        name="gla_fwd_out",
    )(p, p, p, p, wdec, bdec, signs, o_b, p, o_nat, o_mem, x2, w_out, gla_ng, post_g)


NAT_TILE_ROWS = 8
NAT_TILE = NAT_TILE_ROWS * GRID_W
NAT_KEYS = NAT_KH * GRID_W


NAT_BIAS_ROWS = 2 * NAT_KH - 2
PAIR_W = 2 * NAT_DH


def _nat_bias_table(rpb):
    cq = np.arange(GRID_W)[:, None]
    ck = np.arange(GRID_W)[None, :]
    cs = np.clip(cq - NAT_KW // 2, 0, GRID_W - NAT_KW)
    col_in = (ck >= cs) & (ck < cs + NAT_KW)
    dcol = np.clip(ck - cq + (NAT_KW - 1), 0, 2 * NAT_KW - 2)
    onehot = ((dcol[None] == np.arange(2 * NAT_KW - 1)[:, None, None]) & col_in[None]).astype(np.float32)
    tab = jnp.einsum("hrd,dqk->hrqk", rpb.astype(F32), jnp.asarray(onehot),
                     precision=lax.Precision.HIGHEST)
    tab = tab + jnp.asarray(np.where(col_in, 0.0, NEG_BIG).astype(np.float32))
    tab = jnp.concatenate([tab[:, :-1], tab[:, 1:]], axis=-1)
    tab = tab.reshape(NAT_PAIRS, 2, NAT_BIAS_ROWS, GRID_W, PAIR_W)
    return tab.transpose(0, 2, 1, 3, 4).reshape(NAT_PAIRS, NAT_BIAS_ROWS, PAIR_W, PAIR_W)


def _nat_kernel(q_ref, kp_ref, kc_ref, kn_ref, vp_ref, vc_ref, vn_ref, g_ref, bias_ref, o_ref,
                kbuf, vbuf, *, n_rows):
    i = pl.program_id(1)
    kbuf[0:NAT_TILE, :] = kp_ref[...]
    kbuf[NAT_TILE:2 * NAT_TILE, :] = kc_ref[...]
    kbuf[2 * NAT_TILE:3 * NAT_TILE, :] = kn_ref[...]
    ones = jnp.ones((NAT_TILE, PAIR_W), BF16)
    for j, v_ref in enumerate((vp_ref, vc_ref, vn_ref)):
        for p in range(NAT_PAIRS):
            vbuf[j * NAT_TILE:(j + 1) * NAT_TILE, 2 * p * PAIR_W:(2 * p + 1) * PAIR_W] = (
                v_ref[:, p * PAIR_W:(p + 1) * PAIR_W])
            vbuf[j * NAT_TILE:(j + 1) * NAT_TILE, (2 * p + 1) * PAIR_W:(2 * p + 2) * PAIR_W] = ones

    lane = lax.broadcasted_iota(jnp.int32, (GRID_W, PAIR_W), 1)
    first = lane < NAT_DH
    r0 = i * NAT_TILE_ROWS
    for rr in range(NAT_TILE_ROWS):
        r = r0 + rr
        rs = jnp.clip(r - NAT_KH // 2, 0, n_rows - NAT_KH)
        d0 = rs - r + (NAT_KH - 1)
        off = pl.multiple_of((rs - r0 + NAT_TILE_ROWS) * GRID_W, GRID_W)
        rows = slice(rr * GRID_W, (rr + 1) * GRID_W)
        scores = []
        for p in range(NAT_PAIRS):
            lanes = slice(p * PAIR_W, (p + 1) * PAIR_W)
            qp = q_ref[rows, lanes]
            zero = jnp.zeros_like(qp)
            q2 = jnp.concatenate([jnp.where(first, qp, zero), jnp.where(first, zero, qp)], axis=0)
            kk = kbuf[pl.ds(off, NAT_KEYS), lanes]
            s = lax.dot_general(q2, kk, (((1,), (1,)), ((), ())), preferred_element_type=F32)
            bias = jnp.concatenate([bias_ref[p, d0 + 2 * j] for j in range(NAT_KH // 2)], axis=1)
            scores.append(s + bias)
        probs = []
        for s in scores:
            probs.append(jnp.exp(s - jnp.max(s, axis=-1, keepdims=True)).astype(BF16))
        for p, e in enumerate(probs):
            lanes = slice(p * PAIR_W, (p + 1) * PAIR_W)
            vv = vbuf[pl.ds(off, NAT_KEYS), 2 * p * PAIR_W:(2 * p + 2) * PAIR_W]
            pv = jnp.dot(e, vv, preferred_element_type=F32)
            pv = pv[:, 0:PAIR_W] / pv[:, PAIR_W:2 * PAIR_W]
            o = jnp.where(first, pv[0:GRID_W, :], pv[GRID_W:2 * GRID_W, :])
            gate = g_ref[rows, lanes].astype(F32)
            o_ref[rows, lanes] = (o * gate).astype(o_ref.dtype)


def _nat(p, bias_tab, n_seq, n_tok):
    nt = n_tok // NAT_TILE
    n_rows = n_tok // GRID_W
    rows = n_seq * n_tok
    qcol, kcol, vcol, gcol = (c // NAT_W for c in (P_NQ, P_NK, P_NV, P_NG))

    def at(col, shift):
        def index(s, i):
            return (s * nt + jnp.clip(i + shift, 0, nt - 1), col)
        return index

    blk = (NAT_TILE, NAT_W)
    return pl.pallas_call(
        functools.partial(_nat_kernel, n_rows=n_rows),
        grid=(n_seq, nt),
        in_specs=[
            pl.BlockSpec(blk, at(qcol, 0)),
            pl.BlockSpec(blk, at(kcol, -1)), pl.BlockSpec(blk, at(kcol, 0)), pl.BlockSpec(blk, at(kcol, 1)),
            pl.BlockSpec(blk, at(vcol, -1)), pl.BlockSpec(blk, at(vcol, 0)), pl.BlockSpec(blk, at(vcol, 1)),
            pl.BlockSpec(blk, at(gcol, 0)),
            _resident(bias_tab.shape, lambda s, i: (0, 0, 0, 0)),
        ],
        out_specs=pl.BlockSpec(blk, at(0, 0)),
        out_shape=jax.ShapeDtypeStruct((rows, NAT_W), BF16),
        scratch_shapes=[pltpu.VMEM((3 * NAT_TILE, NAT_W), BF16), pltpu.VMEM((3 * NAT_TILE, 2 * NAT_W), BF16)],
        compiler_params=_cparams(2),
        name="nat",
    )(p, p, p, p, p, p, p, p, bias_tab)


MKV_W = MEM_W + 2 * MEM_W


def _memkv_kernel(m_ref, g_ref, w_ref, o_ref):
    x = m_ref[...]
    ms = jnp.mean(x * x, axis=-1, keepdims=True)
    h = ((x * lax.rsqrt(ms + EPS)) * g_ref[...]).astype(BF16)
    kv = jnp.dot(h, w_ref[...], preferred_element_type=F32).astype(o_ref.dtype)
    o_ref[:, 0:MEM_W] = kv[:, 0:MEM_W]
    ones = jnp.ones((N_MEM, MEM_DH), o_ref.dtype)
    for hd in range(MEM_HEADS):
        c0 = MEM_W + 2 * hd * MEM_DH
        o_ref[:, c0:c0 + MEM_DH] = kv[:, MEM_W + hd * MEM_DH:MEM_W + (hd + 1) * MEM_DH]
        o_ref[:, c0 + MEM_DH:c0 + 2 * MEM_DH] = ones


def _memkv(mem2, mem_g, w_kv):
    rows = mem2.shape[0]
    return pl.pallas_call(
        _memkv_kernel,
        grid=(rows // N_MEM,),
        in_specs=[
            pl.BlockSpec((N_MEM, D_MODEL), lambda i: (i, 0)),
            _resident((1, D_MODEL), lambda i: (0, 0)),
            _resident((D_MODEL, 2 * MEM_W), lambda i: (0, 0)),
        ],
        out_specs=pl.BlockSpec((N_MEM, MKV_W), lambda i: (i, 0)),
        out_shape=jax.ShapeDtypeStruct((rows, MKV_W), BF16),
        compiler_params=_cparams(1),
        name="memkv",
    )(mem2, mem_g, w_kv)


def _memattn_kernel(q_ref, g_ref, kv_ref, o_ref):
    for h in range(MEM_HEADS):
        lanes = slice(h * MEM_DH, (h + 1) * MEM_DH)
        kh = kv_ref[:, lanes]
        vh = kv_ref[:, MEM_W + 2 * h * MEM_DH:MEM_W + 2 * (h + 1) * MEM_DH]
        s = lax.dot_general(q_ref[:, lanes], kh, (((1,), (1,)), ((), ())), preferred_element_type=F32)
        e = jnp.exp(s - jnp.max(s, axis=-1, keepdims=True)).astype(BF16)
        pv = jnp.dot(e, vh, preferred_element_type=F32)
        o = pv[:, 0:MEM_DH] / pv[:, MEM_DH:2 * MEM_DH]
        gate = g_ref[:, lanes].astype(F32)
        o_ref[:, lanes] = (o * gate).astype(o_ref.dtype)


def _memattn(p, mkv, n_seq, n_tok, tq):
    nt = n_tok // tq
    rows = n_seq * n_tok
    return pl.pallas_call(
        _memattn_kernel,
        grid=(n_seq, nt),
        in_specs=[
            pl.BlockSpec((tq, MEM_W), lambda s, i: (s * nt + i, P_MQ // MEM_W)),
            pl.BlockSpec((tq, MEM_W), lambda s, i: (s * nt + i, P_MG // MEM_W)),
            pl.BlockSpec((N_MEM, MKV_W), lambda s, i: (s, 0)),
        ],
        out_specs=pl.BlockSpec((tq, MEM_W), lambda s, i: (s * nt + i, 0)),
        out_shape=jax.ShapeDtypeStruct((rows, MEM_W), BF16),
        compiler_params=_cparams(2),
        name="memattn",
    )(p, p, mkv)


def _choose_tile(n, pref):
    t = pref
    while n % t:
        t //= 2
    return t


def _prepare_weights(w_in, gw_f, gb_f, gw_b, gb_b, w_mem_kv, w_out):
    offs = np.cumsum([0, GLA_QK, GLA_QK, GLA_V, GLA_V, GLA_RANK, GLA_RANK,
                      NAT_W, NAT_W, NAT_W, NAT_W, MEM_W, MEM_W])
    w16 = w_in.astype(BF16)
    seg = [w16[:, offs[k]:offs[k + 1]] for k in range(12)]
    gq, gk, gv, gg, lrf, lrb, nq, nk, nv, ng, mq, mg = seg
    pad = jnp.zeros((D_MODEL, LR_PAD - 2 * GLA_RANK), BF16)
    w_in_p = jnp.concatenate([gq, gk, gv, gg, nq, nk, nv, ng, mq, mg, lrf, lrb, pad], axis=1)
    wdec = jnp.zeros((LR_PAD, 2 * GLA_QK), F32)
    wdec = wdec.at[0:GLA_RANK, 0:GLA_QK].set(gw_f)
    wdec = wdec.at[GLA_RANK:2 * GLA_RANK, GLA_QK:].set(gw_b)
    bdec = jnp.concatenate([gb_f, gb_b]).reshape(1, 2 * GLA_QK).astype(F32)
    return w_in_p, wdec.astype(BF16), bdec, w_mem_kv.astype(BF16), w_out.astype(BF16)


def _trunk(x, mkv, pre_g, w_in_p, wdec, bdec, gla_ng, bias_tab, w_out, post_g):
    n_seq, n_tok, _ = x.shape
    rows = n_seq * n_tok
    x2 = x.reshape(rows, D_MODEL)
    t_gla = _choose_tile(n_tok, 256)
    n_chunks = 2 if n_tok % (2 * t_gla) == 0 else 1
    p, o_b = _proj_gla_bwd(x2, pre_g, w_in_p, wdec, bdec, n_seq, n_tok, t_gla, n_chunks)
    o_nat = _nat(p, bias_tab, n_seq, n_tok)
    o_mem = _memattn(p, mkv, n_seq, n_tok, _choose_tile(n_tok, 512))
    y = _gla_fwd_out(p, o_b, o_nat, o_mem, x2, wdec, bdec, w_out, gla_ng, post_g, n_seq, n_tok, t_gla, n_chunks)
    return y.reshape(x.shape)


def kernel(x_prompt, x_sample, mem_prompt, mem_sample, pre_norm_g, w_in, gla_w_fwd, gla_b_fwd, gla_w_bwd,
           gla_b_bwd, gla_norm_g, nat_rpb, mem_norm_g, w_mem_kv, w_out, post_norm_g):
    assert pre_norm_g.shape[0] == 1, "single-layer trunk"
    w_in_p, wdec, bdec, w_kv, w_o = _prepare_weights(
        w_in[0], gla_w_fwd[0], gla_b_fwd[0], gla_w_bwd[0], gla_b_bwd[0], w_mem_kv[0], w_out[0])
    pre_g = pre_norm_g[0].reshape(1, D_MODEL)
    post_g = post_norm_g[0].reshape(1, D_MODEL)
    gla_ng = gla_norm_g[0].reshape(1, GLA_DV)
    mem_g = mem_norm_g[0].reshape(1, D_MODEL)
    bias_tab = _nat_bias_table(nat_rpb[0])

    n_p = mem_prompt.shape[0]
    mem_all = jnp.concatenate([mem_prompt, mem_sample], axis=0).reshape(-1, D_MODEL)
    mkv = _memkv(mem_all, mem_g, w_kv)
    mkv_p, mkv_s = mkv[:n_p * N_MEM], mkv[n_p * N_MEM:]

    run = functools.partial(_trunk, pre_g=pre_g, w_in_p=w_in_p, wdec=wdec, bdec=bdec, gla_ng=gla_ng,
                            bias_tab=bias_tab, w_out=w_o, post_g=post_g)
    return (run(x_prompt, mkv_p), run(x_sample, mkv_s))
```

```python
import functools

import numpy as np
import jax
import jax.numpy as jnp
from jax import lax
from jax.experimental import pallas as pl
from jax.experimental.pallas import tpu as pltpu

F32 = jnp.float32
BF16 = jnp.bfloat16

D_MODEL = 1024
N_MEM = 256
GRID_W = 64
EPS = 1e-6

GLA_HEADS = 4
GLA_DK = 128
GLA_DV = 256
GLA_RANK = 16
GLA_TAU = 16.0
GLA_QK = GLA_HEADS * GLA_DK
GLA_V = GLA_HEADS * GLA_DV

NAT_HEADS = 8
NAT_DH = 64
NAT_KH = 8
NAT_KW = 16
NAT_W = NAT_HEADS * NAT_DH
NAT_PAIRS = NAT_HEADS // 2

MEM_HEADS = 4
MEM_DH = 128
MEM_W = MEM_HEADS * MEM_DH

MIX_W = GLA_V + NAT_W + MEM_W

COL_BLK = 512
P_GQ, P_GK, P_GV, P_GG = 0, 512, 1024, 2048
P_NQ, P_NK, P_NV, P_NG = 3072, 3584, 4096, 4608
P_MQ, P_MG = 5120, 5632
P_LR = 6144
LR_PAD = 128
P_W = P_LR + LR_PAD

NEG_BIG = -1e30

VMEM_LIMIT = 56 * 1024 * 1024


def _cparams(n_axes):
    return pltpu.CompilerParams(
        dimension_semantics=("arbitrary",) * n_axes,
        vmem_limit_bytes=VMEM_LIMIT,
    )


def _resident(shape, index_map):
    return pl.BlockSpec(shape, index_map, pipeline_mode=pl.Buffered(1))


def _silu(g):
    return g * jax.nn.sigmoid(g)


def _normed_input(x_ref, g_ref):
    x = x_ref[...]
    ms = jnp.mean(x * x, axis=-1, keepdims=True)
    return ((x * lax.rsqrt(ms + EPS)) * g_ref[...]).astype(BF16)


def _gla_input_col(c0):
    if c0 < P_GV + GLA_V:
        return c0
    return 2 * GLA_QK + GLA_V if c0 == P_LR else None


def _project_chunks(h, w_ref, o_ref, chunks, gla_ref=None):
    for c0, c1, post in chunks:
        acc = jnp.dot(h, w_ref[:, c0:c1], preferred_element_type=F32)
        if post == "gate":
            acc = _silu(acc)
        elif post is not None:
            acc = acc * post
        val = acc.astype(BF16)
        o_ref[:, c0:c1] = val
        g0 = _gla_input_col(c0)
        if gla_ref is not None and g0 is not None:
            gla_ref[:, g0:g0 + (c1 - c0)] = val


def _proj_scales():
    post = {P_GQ: GLA_DK ** -0.5, P_NQ: LOG2_E * NAT_DH ** -0.5, P_MQ: LOG2_E * MEM_DH ** -0.5,
            P_GG: "gate", P_GG + COL_BLK: "gate", P_NG: "gate", P_MG: "gate"}
    chunks = [(c0, c0 + COL_BLK, post.get(c0)) for c0 in range(0, P_LR, COL_BLK)]
    return tuple(chunks + [(P_LR, P_W, None)])


def _block_row_bcast(b, blk, ridx):
    t = b.shape[0]
    if blk >= 8:
        pieces = []
        for b0 in range(0, t, blk):
            row = b[b0 + ridx:b0 + ridx + 1, :]
            pieces.append(jnp.broadcast_to(row, (blk, b.shape[1])))
        return pieces[0] if len(pieces) == 1 else jnp.concatenate(pieces, axis=0)
    m = lax.broadcasted_iota(jnp.int32, b.shape, 0) & (blk - 1)
    out = b
    for pos in range(blk):
        if pos == ridx:
            continue
        shifted = pltpu.roll(b, (pos - ridx) % t, axis=0)
        out = jnp.where(m == pos, shifted, out)
    return out


class _Window:
    def __init__(self, ref, rows, cols):
        self.ref, self.rows, self.cols, self.dtype = ref, rows, cols, ref.dtype

    def _abs(self, idx):
        idx = (slice(None), slice(None)) if idx is Ellipsis else idx
        out = []
        for sl, base in zip(idx, (self.rows, self.cols)):
            lo = base.start + (sl.start or 0)
            hi = base.stop if sl.stop is None else base.start + sl.stop
            out.append(slice(lo, hi))
        return tuple(out)

    def __getitem__(self, idx):
        return self.ref[self._abs(idx)]

    def __setitem__(self, idx, value):
        self.ref[self._abs(idx)] = value


def _log_sigmoid(z):
    return jnp.minimum(z, 0.0) - jnp.log(1.0 + jnp.exp(-jnp.abs(z)))


LOG2_E = 1.4426950408889634


def _gla_decay(lr_ref, wdec_ref, bdec_ref, *, reverse, t):
    d = 1 if reverse else 0
    z = jnp.dot(lr_ref[...], wdec_ref[:, d * GLA_QK:(d + 1) * GLA_QK], preferred_element_type=F32)
    z = z + bdec_ref[:, d * GLA_QK:(d + 1) * GLA_QK]
    la = _log_sigmoid(z) * (LOG2_E / GLA_TAU)

    ri = lax.broadcasted_iota(jnp.int32, (t, t), 0)
    ci = lax.broadcasted_iota(jnp.int32, (t, t), 1)
    tri = jnp.where((ci >= ri) if reverse else (ci <= ri), 1.0, 0.0).astype(BF16)
    la_hi = la.astype(BF16)
    la_lo = (la - la_hi.astype(F32)).astype(BF16)
    b = (jnp.dot(tri, la_hi, preferred_element_type=F32)
         + jnp.dot(tri, la_lo, preferred_element_type=F32))
    return b


def _gla_role_signs(t, reverse):
    rows = np.arange(t)
    signs = []
    s = t // 2
    while s >= 1:
        later = (rows & (2 * s - 1)) >= s
        signs.append(np.where(later != reverse, 1.0, -1.0))
        s //= 2
    return np.ascontiguousarray(np.broadcast_to(np.stack(signs)[:, :, None], (len(signs), t, GLA_DK)),
                                dtype=np.float32)


def _gla_factors(b, q_ref, k_ref, sign_ref, *, reverse, t):
    q_all = q_ref[...]
    k_all = k_ref[...]
    b_last = b[0:1, :] if reverse else b[t - 1:t, :]

    levels = []
    s = t // 2
    while s >= 1:
        ridx = s if reverse else s - 1
        dist = b - _block_row_bcast(b, 2 * s, ridx)
        sign = sign_ref[len(levels)]
        neg = jnp.concatenate([dist[:, h * GLA_DK:(h + 1) * GLA_DK] * sign for h in range(GLA_HEADS)], axis=1)
        e_l = jnp.exp2(neg).astype(BF16)
        levels.append((s.bit_length() - 1, q_all * e_l, k_all * e_l))
        s //= 2
    return dict(
        q=q_all, k=k_all, levels=levels,
        q_in=q_all * jnp.exp2(b).astype(BF16),
        k_out=k_all * jnp.exp2(b_last - b).astype(BF16),
        e_all=jnp.exp2(b_last))


def _gla_head_intra(pre, v_ref, h, *, reverse, t):
    half = t // 2
    rh = lax.broadcasted_iota(jnp.int32, (half, half), 0)
    ch = lax.broadcasted_iota(jnp.int32, (half, half), 1)
    xor_f = (rh ^ ch).astype(F32)
    level_of = lax.shift_right_logical(pltpu.bitcast(xor_f, jnp.int32), 23) - 127
    level_of = jnp.where((ch > rh) if reverse else (ch < rh), level_of, -1)

    nt_dims = (((1,), (1,)), ((), ()))
    early = slice(half, t) if reverse else slice(0, half)
    late = slice(0, half) if reverse else slice(half, t)
    kl = slice(h * GLA_DK, (h + 1) * GLA_DK)
    vl = slice(h * GLA_DV, (h + 1) * GLA_DV)
    vh = v_ref[:, vl]
    levels = pre["levels"]

    within = {}
    for rows in (early, late):
        qk = pre["q"][rows, kl].astype(F32) * pre["k"][rows, kl].astype(F32)
        att = jnp.where(rh == ch, jnp.sum(qk, axis=-1, keepdims=True), 0.0)
        for lvl, qt, kt in levels[1:]:
            p = lax.dot_general(qt[rows, kl], kt[rows, kl], nt_dims, preferred_element_type=F32)
            att = jnp.where(level_of == lvl, p, att)
        within[rows.start] = att.astype(BF16)
    _, qt, kt = levels[0]
    cross = lax.dot_general(qt[late, kl], kt[early, kl], nt_dims, preferred_element_type=F32).astype(BF16)
    blocks = [within[late.start], cross] if reverse else [cross, within[late.start]]
    o_late = jnp.dot(jnp.concatenate(blocks, axis=1), vh, preferred_element_type=F32)
    o_early = jnp.dot(within[early.start], vh[early], preferred_element_type=F32)
    return ((early, o_early), (late, o_late)), vh


def _gla_head_inter(pre, intra, o_ref, s_ref, h):
    parts, vh = intra
    kl = slice(h * GLA_DK, (h + 1) * GLA_DK)
    vl = slice(h * GLA_DV, (h + 1) * GLA_DV)
    st = s_ref[h]
    st_b = st.astype(BF16)
    for rows, o in parts:
        o = o + lax.dot_general(pre["q_in"][rows, kl], st_b, (((1,), (1,)), ((), ())),
                                preferred_element_type=F32)
        o_ref[rows, vl] = o.astype(o_ref.dtype)

    upd = lax.dot_general(vh, pre["k_out"][:, kl], (((0,), (0,)), ((), ())), preferred_element_type=F32)
    s_ref[h] = st * pre["e_all"][:, kl] + upd


GLA_IN_W = 2 * GLA_QK + GLA_V + LR_PAD


def _proj_gla_bwd_kernel(x_ref, g_ref, w_ref, wdec, bdec, sign_ref, p_ref, ob_ref, s_ref, even_ref, odd_ref, *,
                         scales, t, nc):
    step = pl.program_id(1)

    @pl.when(step == 0)
    def _():
        s_ref[...] = jnp.zeros_like(s_ref)
        odd_ref[...] = jnp.zeros_like(odd_ref)

    body = functools.partial(_proj_gla_bwd_step, x_ref, g_ref, w_ref, wdec, bdec, sign_ref, p_ref, ob_ref, s_ref,
                             scales=scales, t=t, nc=nc)
    pl.when(step % 2 == 0)(lambda: body(odd_ref, even_ref))
    pl.when(step % 2 == 1)(lambda: body(even_ref, odd_ref))


def _proj_gla_bwd_step(x_ref, g_ref, w_ref, wdec, bdec, sign_ref, p_ref, ob_ref, s_ref, pend_ref, next_ref, *,
                       scales, t, nc):
    order = list(reversed(range(nc)))
    rows_of = {c: slice(c * t, (c + 1) * t) for c in order}

    def pending(c, c0, c1):
        return _Window(pend_ref, rows_of[c], slice(c0, c1))

    b = {c: _gla_decay(pending(c, 2 * GLA_QK + GLA_V, GLA_IN_W), wdec, bdec, reverse=True, t=t)
         for c in order}
    hx = _normed_input(x_ref, g_ref)
    n_slots = nc * GLA_HEADS
    per_slot = max(1, (len(scales) - 4) // n_slots)
    n_first = len(scales) - per_slot * n_slots
    _project_chunks(hx, w_ref, p_ref, scales[:n_first], next_ref)
    pre = {c: _gla_factors(b[c], pending(c, 0, GLA_QK), pending(c, GLA_QK, 2 * GLA_QK), sign_ref,
                           reverse=True, t=t) for c in order}
    slot = 0
    intra = {}
    for h in range(GLA_HEADS):
        for c in order:
            _project_chunks(hx, w_ref, p_ref, scales[n_first + slot * per_slot:n_first + (slot + 1) * per_slot],
                            next_ref)
            slot += 1
            intra[c, h] = _gla_head_intra(pre[c], pending(c, 2 * GLA_QK, 2 * GLA_QK + GLA_V), h,
                                          reverse=True, t=t)
    for c in order:
        o_win = _Window(ob_ref, rows_of[c], slice(0, GLA_V))
        for h in range(GLA_HEADS):
            _gla_head_inter(pre[c], intra[c, h], o_win, s_ref, h)


def _proj_gla_bwd(x2, pre_g, w_in_p, wdec, bdec, n_seq, n_tok, t, nc):
    tb = t * nc
    nt = n_tok // tb
    rows = n_seq * n_tok
    assert (P_GQ, P_GK, P_GV) == (0, GLA_QK, 2 * GLA_QK)
    signs = jnp.asarray(_gla_role_signs(t, reverse=True))

    def proj_blk(s, i):
        return (s * nt + nt - 1 - jnp.minimum(i, nt - 1), 0)

    def gla_blk(s, i):
        return (s * nt + nt - 1 - jnp.maximum(i - 1, 0), 0)

    return pl.pallas_call(
        functools.partial(_proj_gla_bwd_kernel, scales=_proj_scales(), t=t, nc=nc),
        grid=(n_seq, nt + 1),
        in_specs=[
            pl.BlockSpec((tb, D_MODEL), proj_blk),
            _resident((1, D_MODEL), lambda s, i: (0, 0)),
            _resident((D_MODEL, P_W), lambda s, i: (0, 0)),
            _resident((LR_PAD, 2 * GLA_QK), lambda s, i: (0, 0)),
            _resident((1, 2 * GLA_QK), lambda s, i: (0, 0)),
            _resident(signs.shape, lambda s, i: (0, 0, 0)),
        ],
        out_specs=[pl.BlockSpec((tb, P_W), proj_blk), pl.BlockSpec((tb, GLA_V), gla_blk)],
        out_shape=[jax.ShapeDtypeStruct((rows, P_W), BF16), jax.ShapeDtypeStruct((rows, GLA_V), BF16)],
        scratch_shapes=[pltpu.VMEM((GLA_HEADS, GLA_DV, GLA_DK), F32),
                        pltpu.VMEM((tb, GLA_IN_W), BF16), pltpu.VMEM((tb, GLA_IN_W), BF16)],
        compiler_params=_cparams(2),
        name="proj_gla_bwd",
    )(x2, pre_g, w_in_p, wdec, bdec, signs)


def _gla_fwd_out_kernel(*refs, t, nc, n_rows):
    s_ref, even_ref, odd_ref = refs[-7:-4]
    step = pl.program_id(1)

    @pl.when(step == 0)
    def _():
        s_ref[...] = jnp.zeros_like(s_ref)
        odd_ref[...] = jnp.zeros_like(odd_ref)

    body = functools.partial(_gla_fwd_out_step, refs[:-7], s_ref, refs[-4:], t=t, nc=nc, n_rows=n_rows)
    pl.when(step % 2 == 0)(lambda: body(odd_ref, even_ref))
    pl.when(step % 2 == 1)(lambda: body(even_ref, odd_ref))


def _gla_fwd_out_step(io_refs, s_ref, work_refs, of_pend, of_new, *, t, nc, n_rows):
    (q_ref, k_ref, v_ref, lr_ref, wdec, bdec, sign_ref, ob_ref, gg_ref,
     nq_ref, nkp_ref, nkc_ref, nkn_ref, nvp_ref, nvc_ref, nvn_ref, ng_ref, bias_ref, mq_ref, mg_ref, mkv_ref,
     x_ref, w_ref, gng_ref, post_ref, y_ref) = io_refs
    kbuf, vbuf, nat_ref, mem_ref = work_refs

    chunk_rows = [slice(c * t, (c + 1) * t) for c in range(nc)]
    b = [_gla_decay(_Window(lr_ref, r, slice(0, LR_PAD)), wdec, bdec, reverse=False, t=t) for r in chunk_rows]

    prev_block = jnp.maximum(pl.program_id(1) - 1, 0)
    _nat_block(prev_block, nq_ref, nkp_ref, nkc_ref, nkn_ref, nvp_ref, nvc_ref, nvn_ref, ng_ref, bias_ref, nat_ref,
               kbuf, vbuf, n_rows=n_rows)
    _memattn_block(mq_ref, mg_ref, mkv_ref, mem_ref)

    acc = jnp.dot(nat_ref[...], w_ref[GLA_V:GLA_V + NAT_W, :], preferred_element_type=F32)
    acc = acc + jnp.dot(mem_ref[...], w_ref[GLA_V + NAT_W:MIX_W, :], preferred_element_type=F32)
    for h in range(GLA_HEADS):
        lanes = slice(h * GLA_DV, (h + 1) * GLA_DV)
        o = of_pend[:, lanes].astype(F32) + ob_ref[:, lanes].astype(F32)
        ms = jnp.mean(o * o, axis=-1, keepdims=True)
        on = (o * lax.rsqrt(ms + EPS)) * gng_ref[...]
        og = (on * gg_ref[:, lanes].astype(F32)).astype(BF16)
        acc = acc + jnp.dot(og, w_ref[lanes, :], preferred_element_type=F32)
    ms = jnp.mean(acc * acc, axis=-1, keepdims=True)
    y_ref[...] = x_ref[...] + (acc * lax.rsqrt(ms + EPS)) * post_ref[...]

    pre = [_gla_factors(b[c], _Window(q_ref, r, slice(0, GLA_QK)), _Window(k_ref, r, slice(0, GLA_QK)), sign_ref,
                        reverse=False, t=t)
           for c, r in enumerate(chunk_rows)]
    intra = {}
    for h in range(GLA_HEADS):
        for c, r in enumerate(chunk_rows):
            intra[c, h] = _gla_head_intra(pre[c], _Window(v_ref, r, slice(0, GLA_V)), h, reverse=False, t=t)
    for c, r in enumerate(chunk_rows):
        for h in range(GLA_HEADS):
            _gla_head_inter(pre[c], intra[c, h], _Window(of_new, r, slice(0, GLA_V)), s_ref, h)


def _gla_fwd_out(p, o_b, mkv, bias_tab, x2, wdec, bdec, w_out, gla_ng, post_g, n_seq, n_tok, t, nc):
    tb = t * nc
    assert tb == NAT_TILE, "the attention branches work on 8 grid rows per step"
    nt = n_tok // tb
    rows = n_seq * n_tok
    halo_per_blk = tb // NAT_HALO

    signs = jnp.asarray(_gla_role_signs(t, reverse=False))

    def gla_blk(col):
        return lambda s, i: (s * nt + jnp.minimum(i, nt - 1), col)

    def out_blk(col):
        return lambda s, i: (s * nt + jnp.maximum(i - 1, 0), col)

    def halo_blk(col, side):
        def index(s, i):
            j = jnp.maximum(i - 1, 0) * halo_per_blk
            j = jnp.maximum(j - 1, 0) if side < 0 else jnp.minimum(j + halo_per_blk, nt * halo_per_blk - 1)
            return (s * nt * halo_per_blk + j, col)
        return index

    nat_blk = (tb, NAT_W)
    halo = (NAT_HALO, NAT_W)
    nq_col, nk_col, nv_col, ng_col = (c // NAT_W for c in (P_NQ, P_NK, P_NV, P_NG))

    return pl.pallas_call(
        functools.partial(_gla_fwd_out_kernel, t=t, nc=nc, n_rows=n_tok // GRID_W),
        grid=(n_seq, nt + 1),
        in_specs=[
            pl.BlockSpec((tb, GLA_QK), gla_blk(P_GQ // GLA_QK)),
            pl.BlockSpec((tb, GLA_QK), gla_blk(P_GK // GLA_QK)),
            pl.BlockSpec((tb, GLA_V), gla_blk(P_GV // GLA_V)),
            pl.BlockSpec((tb, LR_PAD), gla_blk(P_LR // LR_PAD)),
            _resident((LR_PAD, 2 * GLA_QK), lambda s, i: (0, 0)),
            _resident((1, 2 * GLA_QK), lambda s, i: (0, 0)),
            _resident(signs.shape, lambda s, i: (0, 0, 0)),
            pl.BlockSpec((tb, GLA_V), out_blk(0)),
            pl.BlockSpec((tb, GLA_V), out_blk(P_GG // GLA_V)),
            pl.BlockSpec(nat_blk, out_blk(nq_col)),
            pl.BlockSpec(halo, halo_blk(nk_col, -1)), pl.BlockSpec(nat_blk, out_blk(nk_col)),
            pl.BlockSpec(halo, halo_blk(nk_col, 1)),
            pl.BlockSpec(halo, halo_blk(nv_col, -1)), pl.BlockSpec(nat_blk, out_blk(nv_col)),
            pl.BlockSpec(halo, halo_blk(nv_col, 1)),
            pl.BlockSpec(nat_blk, out_blk(ng_col)),
            _resident(bias_tab.shape, lambda s, i: (0, 0, 0, 0)),
            pl.BlockSpec((tb, MEM_W), out_blk(P_MQ // MEM_W)),
            pl.BlockSpec((tb, MEM_W), out_blk(P_MG // MEM_W)),
            pl.BlockSpec((N_MEM, MKV_W), lambda s, i: (s, 0)),
            pl.BlockSpec((tb, D_MODEL), out_blk(0)),
            _resident((MIX_W, D_MODEL), lambda s, i: (0, 0)),
            _resident((1, GLA_DV), lambda s, i: (0, 0)),
            _resident((1, D_MODEL), lambda s, i: (0, 0)),
        ],
        out_specs=pl.BlockSpec((tb, D_MODEL), out_blk(0)),
        out_shape=jax.ShapeDtypeStruct((rows, D_MODEL), F32),
        scratch_shapes=[pltpu.VMEM((GLA_HEADS, GLA_DV, GLA_DK), F32),
                        pltpu.VMEM((tb, GLA_V), BF16), pltpu.VMEM((tb, GLA_V), BF16),
                        pltpu.VMEM((NAT_BUF, NAT_W), BF16), pltpu.VMEM((NAT_BUF, 2 * NAT_W), BF16),
                        pltpu.VMEM((tb, NAT_W), BF16), pltpu.VMEM((tb, MEM_W), BF16)],
        compiler_params=_cparams(2),
        name="gla_fwd_out",
    )(p, p, p, p, wdec, bdec, signs, o_b, p, p, p, p, p, p, p, p, p, bias_tab, p, p, mkv, x2, w_out, gla_ng, post_g)


NAT_TILE_ROWS = 8
NAT_TILE = NAT_TILE_ROWS * GRID_W
NAT_KEYS = NAT_KH * GRID_W


NAT_BIAS_ROWS = 2 * NAT_KH - 2
PAIR_W = 2 * NAT_DH


def _nat_bias_table(rpb):
    cq = np.arange(GRID_W)[:, None]
    ck = np.arange(GRID_W)[None, :]
    cs = np.clip(cq - NAT_KW // 2, 0, GRID_W - NAT_KW)
    col_in = (ck >= cs) & (ck < cs + NAT_KW)
    dcol = np.clip(ck - cq + (NAT_KW - 1), 0, 2 * NAT_KW - 2)
    onehot = ((dcol[None] == np.arange(2 * NAT_KW - 1)[:, None, None]) & col_in[None]).astype(np.float32)
    tab = jnp.einsum("hrd,dqk->hrqk", rpb.astype(F32), jnp.asarray(onehot),
                     precision=lax.Precision.HIGHEST)
    tab = tab * LOG2_E + jnp.asarray(np.where(col_in, 0.0, NEG_BIG).astype(np.float32))
    tab = jnp.concatenate([tab[:, :-1], tab[:, 1:]], axis=-1)
    tab = tab.reshape(NAT_PAIRS, 2, NAT_BIAS_ROWS, GRID_W, PAIR_W)
    return tab.transpose(0, 2, 1, 3, 4).reshape(NAT_PAIRS, NAT_BIAS_ROWS, PAIR_W, PAIR_W)


NAT_HALO = (NAT_KH // 2) * GRID_W
NAT_BUF = NAT_TILE + 2 * NAT_HALO


def _nat_block(j, q_ref, kp_ref, kc_ref, kn_ref, vp_ref, vc_ref, vn_ref, g_ref, bias_ref, o_ref, kbuf, vbuf,
               *, n_rows):
    kbuf[0:NAT_HALO, :] = kp_ref[...]
    kbuf[NAT_HALO:NAT_HALO + NAT_TILE, :] = kc_ref[...]
    kbuf[NAT_HALO + NAT_TILE:NAT_BUF, :] = kn_ref[...]
    for r0, r1, v_ref in ((0, NAT_HALO, vp_ref), (NAT_HALO, NAT_HALO + NAT_TILE, vc_ref),
                          (NAT_HALO + NAT_TILE, NAT_BUF, vn_ref)):
        ones = jnp.ones((r1 - r0, PAIR_W), BF16)
        for p in range(NAT_PAIRS):
            vbuf[r0:r1, 2 * p * PAIR_W:(2 * p + 1) * PAIR_W] = v_ref[:, p * PAIR_W:(p + 1) * PAIR_W]
            vbuf[r0:r1, (2 * p + 1) * PAIR_W:(2 * p + 2) * PAIR_W] = ones

    lane = lax.broadcasted_iota(jnp.int32, (GRID_W, PAIR_W), 1)
    first = lane < NAT_DH
    row0 = j * NAT_TILE_ROWS
    for rr in range(NAT_TILE_ROWS):
        r = row0 + rr
        rs = jnp.clip(r - NAT_KH // 2, 0, n_rows - NAT_KH)
        d0 = rs - r + (NAT_KH - 1)
        off = pl.multiple_of((rs - row0 + NAT_KH // 2) * GRID_W, GRID_W)
        rows = slice(rr * GRID_W, (rr + 1) * GRID_W)
        scores = []
        for p in range(NAT_PAIRS):
            lanes = slice(p * PAIR_W, (p + 1) * PAIR_W)
            qp = q_ref[rows, lanes]
            zero = jnp.zeros_like(qp)
            q2 = jnp.concatenate([jnp.where(first, qp, zero), jnp.where(first, zero, qp)], axis=0)
            kk = kbuf[pl.ds(off, NAT_KEYS), lanes]
            s = lax.dot_general(q2, kk, (((1,), (1,)), ((), ())), preferred_element_type=F32)
            bias = jnp.concatenate([bias_ref[p, d0 + 2 * jj] for jj in range(NAT_KH // 2)], axis=1)
            scores.append(s + bias)
        probs = []
        for s in scores:
            probs.append(jnp.exp2(s - jnp.max(s, axis=-1, keepdims=True)).astype(BF16))
        for p, e in enumerate(probs):
            lanes = slice(p * PAIR_W, (p + 1) * PAIR_W)
            vv = vbuf[pl.ds(off, NAT_KEYS), 2 * p * PAIR_W:(2 * p + 2) * PAIR_W]
            pv = jnp.dot(e, vv, preferred_element_type=F32)
            pv = pv[:, 0:PAIR_W] / pv[:, PAIR_W:2 * PAIR_W]
            o = jnp.where(first, pv[0:GRID_W, :], pv[GRID_W:2 * GRID_W, :])
            gate = g_ref[rows, lanes].astype(F32)
            o_ref[rows, lanes] = (o * gate).astype(o_ref.dtype)


MKV_W = MEM_W + 2 * MEM_W


def _memkv_kernel(m_ref, g_ref, w_ref, o_ref):
    x = m_ref[...]
    ms = jnp.mean(x * x, axis=-1, keepdims=True)
    h = ((x * lax.rsqrt(ms + EPS)) * g_ref[...]).astype(BF16)
    kv = jnp.dot(h, w_ref[...], preferred_element_type=F32).astype(o_ref.dtype)
    o_ref[:, 0:MEM_W] = kv[:, 0:MEM_W]
    ones = jnp.ones((N_MEM, MEM_DH), o_ref.dtype)
    for hd in range(MEM_HEADS):
        c0 = MEM_W + 2 * hd * MEM_DH
        o_ref[:, c0:c0 + MEM_DH] = kv[:, MEM_W + hd * MEM_DH:MEM_W + (hd + 1) * MEM_DH]
        o_ref[:, c0 + MEM_DH:c0 + 2 * MEM_DH] = ones


def _memkv(mem2, mem_g, w_kv):
    rows = mem2.shape[0]
    return pl.pallas_call(
        _memkv_kernel,
        grid=(rows // N_MEM,),
        in_specs=[
            pl.BlockSpec((N_MEM, D_MODEL), lambda i: (i, 0)),
            _resident((1, D_MODEL), lambda i: (0, 0)),
            _resident((D_MODEL, 2 * MEM_W), lambda i: (0, 0)),
        ],
        out_specs=pl.BlockSpec((N_MEM, MKV_W), lambda i: (i, 0)),
        out_shape=jax.ShapeDtypeStruct((rows, MKV_W), BF16),
        compiler_params=_cparams(1),
        name="memkv",
    )(mem2, mem_g, w_kv)


def _memattn_block(q_ref, g_ref, kv_ref, o_ref):
    for h in range(MEM_HEADS):
        lanes = slice(h * MEM_DH, (h + 1) * MEM_DH)
        kh = kv_ref[:, lanes]
        vh = kv_ref[:, MEM_W + 2 * h * MEM_DH:MEM_W + 2 * (h + 1) * MEM_DH]
        s = lax.dot_general(q_ref[:, lanes], kh, (((1,), (1,)), ((), ())), preferred_element_type=F32)
        e = jnp.exp2(s - jnp.max(s, axis=-1, keepdims=True)).astype(BF16)
        pv = jnp.dot(e, vh, preferred_element_type=F32)
        o = pv[:, 0:MEM_DH] / pv[:, MEM_DH:2 * MEM_DH]
        gate = g_ref[:, lanes].astype(F32)
        o_ref[:, lanes] = (o * gate).astype(o_ref.dtype)


def _choose_tile(n, pref):
    t = pref
    while n % t:
        t //= 2
    return t


def _prepare_weights(w_in, gw_f, gb_f, gw_b, gb_b, w_mem_kv, w_out):
    offs = np.cumsum([0, GLA_QK, GLA_QK, GLA_V, GLA_V, GLA_RANK, GLA_RANK,
                      NAT_W, NAT_W, NAT_W, NAT_W, MEM_W, MEM_W])
    w16 = w_in.astype(BF16)
    seg = [w16[:, offs[k]:offs[k + 1]] for k in range(12)]
    gq, gk, gv, gg, lrf, lrb, nq, nk, nv, ng, mq, mg = seg
    pad = jnp.zeros((D_MODEL, LR_PAD - 2 * GLA_RANK), BF16)
    w_in_p = jnp.concatenate([gq, gk, gv, gg, nq, nk, nv, ng, mq, mg, lrf, lrb, pad], axis=1)
    wdec = jnp.zeros((LR_PAD, 2 * GLA_QK), F32)
    wdec = wdec.at[0:GLA_RANK, 0:GLA_QK].set(gw_f)
    wdec = wdec.at[GLA_RANK:2 * GLA_RANK, GLA_QK:].set(gw_b)
    bdec = jnp.concatenate([gb_f, gb_b]).reshape(1, 2 * GLA_QK).astype(F32)
    return w_in_p, wdec.astype(BF16), bdec, w_mem_kv.astype(BF16), w_out.astype(BF16)


def _trunk(x, mkv, pre_g, w_in_p, wdec, bdec, gla_ng, bias_tab, w_out, post_g):
    n_seq, n_tok, _ = x.shape
    rows = n_seq * n_tok
    x2 = x.reshape(rows, D_MODEL)
    t_gla = _choose_tile(n_tok, 256)
    n_chunks = 2 if n_tok % (2 * t_gla) == 0 else 1
    p, o_b = _proj_gla_bwd(x2, pre_g, w_in_p, wdec, bdec, n_seq, n_tok, t_gla, n_chunks)
    y = _gla_fwd_out(p, o_b, mkv, bias_tab, x2, wdec, bdec, w_out, gla_ng, post_g, n_seq, n_tok, t_gla, n_chunks)
    return y.reshape(x.shape)


def kernel(x_prompt, x_sample, mem_prompt, mem_sample, pre_norm_g, w_in, gla_w_fwd, gla_b_fwd, gla_w_bwd,
           gla_b_bwd, gla_norm_g, nat_rpb, mem_norm_g, w_mem_kv, w_out, post_norm_g):
    assert pre_norm_g.shape[0] == 1, "single-layer trunk"
    w_in_p, wdec, bdec, w_kv, w_o = _prepare_weights(
        w_in[0], gla_w_fwd[0], gla_b_fwd[0], gla_w_bwd[0], gla_b_bwd[0], w_mem_kv[0], w_out[0])
    pre_g = pre_norm_g[0].reshape(1, D_MODEL)
    post_g = post_norm_g[0].reshape(1, D_MODEL)
    gla_ng = gla_norm_g[0].reshape(1, GLA_DV)
    mem_g = mem_norm_g[0].reshape(1, D_MODEL)
    bias_tab = _nat_bias_table(nat_rpb[0])

    n_p = mem_prompt.shape[0]
    mem_all = jnp.concatenate([mem_prompt, mem_sample], axis=0).reshape(-1, D_MODEL)
    mkv = _memkv(mem_all, mem_g, w_kv)
    mkv_p, mkv_s = mkv[:n_p * N_MEM], mkv[n_p * N_MEM:]

    run = functools.partial(_trunk, pre_g=pre_g, w_in_p=w_in_p, wdec=wdec, bdec=bdec, gla_ng=gla_ng,
                            bias_tab=bias_tab, w_out=w_o, post_g=post_g)
    return (run(x_prompt, mkv_p), run(x_sample, mkv_s))
```

```python
import functools

import numpy as np
import jax
import jax.numpy as jnp
from jax import lax
from jax.experimental import pallas as pl
from jax.experimental.pallas import tpu as pltpu

F32 = jnp.float32
BF16 = jnp.bfloat16

D_MODEL = 1024
N_MEM = 256
GRID_W = 64
EPS = 1e-6

GLA_HEADS = 4
GLA_DK = 128
GLA_DV = 256
GLA_RANK = 16
GLA_TAU = 16.0
GLA_QK = GLA_HEADS * GLA_DK
GLA_V = GLA_HEADS * GLA_DV

NAT_HEADS = 8
NAT_DH = 64
NAT_KH = 8
NAT_KW = 16
NAT_W = NAT_HEADS * NAT_DH
NAT_PAIRS = NAT_HEADS // 2

MEM_HEADS = 4
MEM_DH = 128
MEM_W = MEM_HEADS * MEM_DH

MIX_W = GLA_V + NAT_W + MEM_W

COL_BLK = 512
P_GQ, P_GK, P_GV, P_GG = 0, 512, 1024, 2048
P_NQ, P_NK, P_NV, P_NG = 3072, 3584, 4096, 4608
P_MQ, P_MG = 5120, 5632
P_LR = 6144
LR_PAD = 128
P_W = P_LR + LR_PAD

NEG_BIG = -1e30

VMEM_LIMIT = 56 * 1024 * 1024


def _cparams(n_axes):
    return pltpu.CompilerParams(
        dimension_semantics=("arbitrary",) * n_axes,
        vmem_limit_bytes=VMEM_LIMIT,
    )


def _resident(shape, index_map):
    return pl.BlockSpec(shape, index_map, pipeline_mode=pl.Buffered(1))


def _silu(g):
    return g * jax.nn.sigmoid(g)


def _normed_input(x_ref, g_ref):
    x = x_ref[...]
    ms = jnp.mean(x * x, axis=-1, keepdims=True)
    return ((x * lax.rsqrt(ms + EPS)) * g_ref[...]).astype(BF16)


def _gla_input_col(c0):
    if c0 < P_GV + GLA_V:
        return c0
    return 2 * GLA_QK + GLA_V if c0 == P_LR else None


def _project_chunks(h, w_ref, o_ref, chunks, gla_ref=None):
    for c0, c1, post in chunks:
        acc = jnp.dot(h, w_ref[:, c0:c1], preferred_element_type=F32)
        if post == "gate":
            acc = _silu(acc)
        elif post is not None:
            acc = acc * post
        val = acc.astype(BF16)
        o_ref[:, c0:c1] = val
        g0 = _gla_input_col(c0)
        if gla_ref is not None and g0 is not None:
            gla_ref[:, g0:g0 + (c1 - c0)] = val


def _proj_scales():
    post = {P_GQ: GLA_DK ** -0.5, P_NQ: LOG2_E * NAT_DH ** -0.5, P_MQ: LOG2_E * MEM_DH ** -0.5,
            P_GG: "gate", P_GG + COL_BLK: "gate", P_NG: "gate", P_MG: "gate"}
    chunks = [(c0, c0 + COL_BLK, post.get(c0)) for c0 in range(0, P_LR, COL_BLK)]
    return tuple(chunks + [(P_LR, P_W, None)])


def _block_row_bcast(b, blk, ridx):
    t = b.shape[0]
    if blk >= 8:
        pieces = []
        for b0 in range(0, t, blk):
            row = b[b0 + ridx:b0 + ridx + 1, :]
            pieces.append(jnp.broadcast_to(row, (blk, b.shape[1])))
        return pieces[0] if len(pieces) == 1 else jnp.concatenate(pieces, axis=0)
    m = lax.broadcasted_iota(jnp.int32, b.shape, 0) & (blk - 1)
    out = b
    for pos in range(blk):
        if pos == ridx:
            continue
        shifted = pltpu.roll(b, (pos - ridx) % t, axis=0)
        out = jnp.where(m == pos, shifted, out)
    return out


class _Window:
    def __init__(self, ref, rows, cols):
        self.ref, self.rows, self.cols, self.dtype = ref, rows, cols, ref.dtype

    def _abs(self, idx):
        idx = (slice(None), slice(None)) if idx is Ellipsis else idx
        out = []
        for sl, base in zip(idx, (self.rows, self.cols)):
            lo = base.start + (sl.start or 0)
            hi = base.stop if sl.stop is None else base.start + sl.stop
            out.append(slice(lo, hi))
        return tuple(out)

    def __getitem__(self, idx):
        return self.ref[self._abs(idx)]

    def __setitem__(self, idx, value):
        self.ref[self._abs(idx)] = value


def _log_sigmoid(z):
    return jnp.minimum(z, 0.0) - jnp.log(1.0 + jnp.exp(-jnp.abs(z)))


LOG2_E = 1.4426950408889634


def _gla_log_decay(lr_ref, wdec_ref, bdec_ref, *, reverse):
    d = 1 if reverse else 0
    z = jnp.dot(lr_ref[...], wdec_ref[:, d * GLA_QK:(d + 1) * GLA_QK], preferred_element_type=F32)
    z = z + bdec_ref[:, d * GLA_QK:(d + 1) * GLA_QK]
    return _log_sigmoid(z) * (LOG2_E / GLA_TAU)


def _gla_cumulate(la, *, reverse, t):
    ri = lax.broadcasted_iota(jnp.int32, (t, t), 0)
    ci = lax.broadcasted_iota(jnp.int32, (t, t), 1)
    tri = jnp.where((ci >= ri) if reverse else (ci <= ri), 1.0, 0.0).astype(BF16)
    la_hi = la.astype(BF16)
    la_lo = (la - la_hi.astype(F32)).astype(BF16)
    b = (jnp.dot(tri, la_hi, preferred_element_type=F32)
         + jnp.dot(tri, la_lo, preferred_element_type=F32))
    return b


def _gla_role_signs(t, reverse):
    rows = np.arange(t)
    signs = []
    s = t // 2
    while s >= 1:
        later = (rows & (2 * s - 1)) >= s
        signs.append(np.where(later != reverse, 1.0, -1.0))
        s //= 2
    return np.ascontiguousarray(np.broadcast_to(np.stack(signs)[:, :, None], (len(signs), t, GLA_DK)),
                                dtype=np.float32)


def _gla_factors(b, q_ref, k_ref, sign_ref, *, reverse, t):
    q_all = q_ref[...]
    k_all = k_ref[...]
    b_last = b[0:1, :] if reverse else b[t - 1:t, :]

    levels = []
    s = t // 2
    while s >= 1:
        ridx = s if reverse else s - 1
        dist = b - _block_row_bcast(b, 2 * s, ridx)
        sign = sign_ref[len(levels)]
        neg = jnp.concatenate([dist[:, h * GLA_DK:(h + 1) * GLA_DK] * sign for h in range(GLA_HEADS)], axis=1)
        e_l = jnp.exp2(neg).astype(BF16)
        levels.append((s.bit_length() - 1, q_all * e_l, k_all * e_l))
        s //= 2
    return dict(
        q=q_all, k=k_all, levels=levels,
        q_in=q_all * jnp.exp2(b).astype(BF16),
        k_out=k_all * jnp.exp2(b_last - b).astype(BF16),
        e_all=jnp.exp2(b_last))


def _gla_head_levels(pre, h, *, reverse, t):
    half = t // 2
    rh = lax.broadcasted_iota(jnp.int32, (half, half), 0)
    ch = lax.broadcasted_iota(jnp.int32, (half, half), 1)
    xor_f = (rh ^ ch).astype(F32)
    level_of = lax.shift_right_logical(pltpu.bitcast(xor_f, jnp.int32), 23) - 127
    level_of = jnp.where((ch > rh) if reverse else (ch < rh), level_of, -1)

    nt_dims = (((1,), (1,)), ((), ()))
    early = slice(half, t) if reverse else slice(0, half)
    late = slice(0, half) if reverse else slice(half, t)
    kl = slice(h * GLA_DK, (h + 1) * GLA_DK)
    levels = pre["levels"]

    within = {}
    for rows in (early, late):
        qk = pre["q"][rows, kl].astype(F32) * pre["k"][rows, kl].astype(F32)
        att = jnp.where(rh == ch, jnp.sum(qk, axis=-1, keepdims=True), 0.0)
        for lvl, qt, kt in levels[1:]:
            p = lax.dot_general(qt[rows, kl], kt[rows, kl], nt_dims, preferred_element_type=F32)
            att = jnp.where(level_of == lvl, p, att)
        within[rows.start] = att.astype(BF16)
    _, qt, kt = levels[0]
    cross = lax.dot_general(qt[late, kl], kt[early, kl], nt_dims, preferred_element_type=F32).astype(BF16)
    blocks = [within[late.start], cross] if reverse else [cross, within[late.start]]
    return early, late, within[early.start], jnp.concatenate(blocks, axis=1)


def _gla_head_pv(att, v_ref, h):
    early, late, att_early, att_late = att
    vh = v_ref[:, h * GLA_DV:(h + 1) * GLA_DV]
    o_late = jnp.dot(att_late, vh, preferred_element_type=F32)
    o_early = jnp.dot(att_early, vh[early], preferred_element_type=F32)
    return ((early, o_early), (late, o_late)), vh


def _gla_head_inter(pre, intra, o_ref, s_ref, h):
    parts, vh = intra
    kl = slice(h * GLA_DK, (h + 1) * GLA_DK)
    vl = slice(h * GLA_DV, (h + 1) * GLA_DV)
    st = s_ref[h]
    st_b = st.astype(BF16)
    for rows, o in parts:
        o = o + lax.dot_general(pre["q_in"][rows, kl], st_b, (((1,), (1,)), ((), ())),
                                preferred_element_type=F32)
        o_ref[rows, vl] = o.astype(o_ref.dtype)

    upd = lax.dot_general(vh, pre["k_out"][:, kl], (((0,), (0,)), ((), ())), preferred_element_type=F32)
    s_ref[h] = st * pre["e_all"][:, kl] + upd


GLA_IN_W = 2 * GLA_QK + GLA_V + LR_PAD


def _proj_gla_bwd_kernel(x0_ref, xn_ref, g_ref, w_ref, wdec, bdec, sign_ref, p_ref, ob_ref, s_ref, even_ref, odd_ref,
                         h_even, h_odd, *, scales, t, nc):
    step = pl.program_id(1)

    @pl.when(step == 0)
    def _():
        s_ref[...] = jnp.zeros_like(s_ref)
        odd_ref[...] = jnp.zeros_like(odd_ref)
        h_odd[...] = _normed_input(x0_ref, g_ref)

    body = functools.partial(_proj_gla_bwd_step, xn_ref, g_ref, w_ref, wdec, bdec, sign_ref, p_ref, ob_ref, s_ref,
                             scales=scales, t=t, nc=nc)
    pl.when(step % 2 == 0)(lambda: body(odd_ref, even_ref, h_odd, h_even))
    pl.when(step % 2 == 1)(lambda: body(even_ref, odd_ref, h_even, h_odd))


def _proj_gla_bwd_step(xn_ref, g_ref, w_ref, wdec, bdec, sign_ref, p_ref, ob_ref, s_ref, pend_ref, next_ref,
                       h_ref, h_next_ref, *, scales, t, nc):
    order = list(reversed(range(nc)))
    rows_of = {c: slice(c * t, (c + 1) * t) for c in order}

    def pending(c, c0, c1):
        return _Window(pend_ref, rows_of[c], slice(c0, c1))

    la = {c: _gla_log_decay(pending(c, 2 * GLA_QK + GLA_V, GLA_IN_W), wdec, bdec, reverse=True) for c in order}
    hx = h_ref[...]
    n_slots = nc * GLA_HEADS
    per_slot = max(1, (len(scales) - 4) // n_slots)
    n_first = len(scales) - per_slot * n_slots
    n_cover = 2
    _project_chunks(hx, w_ref, p_ref, scales[:n_cover], next_ref)
    b = {c: _gla_cumulate(la[c], reverse=True, t=t) for c in order}
    _project_chunks(hx, w_ref, p_ref, scales[n_cover:n_first], next_ref)
    pre = {c: _gla_factors(b[c], pending(c, 0, GLA_QK), pending(c, GLA_QK, 2 * GLA_QK), sign_ref,
                           reverse=True, t=t) for c in order}
    intra = {}
    waiting = None
    for slot, (h, c) in enumerate((h, c) for h in range(GLA_HEADS) for c in order):
        _project_chunks(hx, w_ref, p_ref, scales[n_first + slot * per_slot:n_first + (slot + 1) * per_slot],
                        next_ref)
        att = _gla_head_levels(pre[c], h, reverse=True, t=t)
        if waiting is not None:
            intra[waiting[0]] = _gla_head_pv(waiting[1], pending(waiting[0][0], 2 * GLA_QK, 2 * GLA_QK + GLA_V),
                                             waiting[0][1])
        waiting = ((c, h), att)
    intra[waiting[0]] = _gla_head_pv(waiting[1], pending(waiting[0][0], 2 * GLA_QK, 2 * GLA_QK + GLA_V),
                                     waiting[0][1])
    for c in order:
        o_win = _Window(ob_ref, rows_of[c], slice(0, GLA_V))
        for h in range(GLA_HEADS):
            _gla_head_inter(pre[c], intra[c, h], o_win, s_ref, h)
    h_next_ref[...] = _normed_input(xn_ref, g_ref)


def _proj_gla_bwd(x2, pre_g, w_in_p, wdec, bdec, n_seq, n_tok, t, nc):
    tb = t * nc
    nt = n_tok // tb
    rows = n_seq * n_tok
    assert (P_GQ, P_GK, P_GV) == (0, GLA_QK, 2 * GLA_QK)
    signs = jnp.asarray(_gla_role_signs(t, reverse=True))

    def proj_blk(s, i):
        return (s * nt + nt - 1 - jnp.minimum(i, nt - 1), 0)

    def gla_blk(s, i):
        return (s * nt + nt - 1 - jnp.maximum(i - 1, 0), 0)

    return pl.pallas_call(
        functools.partial(_proj_gla_bwd_kernel, scales=_proj_scales(), t=t, nc=nc),
        grid=(n_seq, nt + 1),
        in_specs=[
            pl.BlockSpec((tb, D_MODEL), lambda s, i: (s * nt + nt - 1, 0), pipeline_mode=pl.Buffered(1)),
            pl.BlockSpec((tb, D_MODEL), lambda s, i: proj_blk(s, i + 1)),
            _resident((1, D_MODEL), lambda s, i: (0, 0)),
            _resident((D_MODEL, P_W), lambda s, i: (0, 0)),
            _resident((LR_PAD, 2 * GLA_QK), lambda s, i: (0, 0)),
            _resident((1, 2 * GLA_QK), lambda s, i: (0, 0)),
            _resident(signs.shape, lambda s, i: (0, 0, 0)),
        ],
        out_specs=[pl.BlockSpec((tb, P_W), proj_blk), pl.BlockSpec((tb, GLA_V), gla_blk)],
        out_shape=[jax.ShapeDtypeStruct((rows, P_W), BF16), jax.ShapeDtypeStruct((rows, GLA_V), BF16)],
        scratch_shapes=[pltpu.VMEM((GLA_HEADS, GLA_DV, GLA_DK), F32),
                        pltpu.VMEM((tb, GLA_IN_W), BF16), pltpu.VMEM((tb, GLA_IN_W), BF16),
                        pltpu.VMEM((tb, D_MODEL), BF16), pltpu.VMEM((tb, D_MODEL), BF16)],
        compiler_params=_cparams(2),
        name="proj_gla_bwd",
    )(x2, x2, pre_g, w_in_p, wdec, bdec, signs)


def _gla_fwd_out_kernel(*refs, t, nc, n_rows):
    s_ref, even_ref, odd_ref = refs[-7:-4]
    step = pl.program_id(1)

    @pl.when(step == 0)
    def _():
        s_ref[...] = jnp.zeros_like(s_ref)
        odd_ref[...] = jnp.zeros_like(odd_ref)

    body = functools.partial(_gla_fwd_out_step, refs[:-7], s_ref, refs[-4:], t=t, nc=nc, n_rows=n_rows)
    pl.when(step % 2 == 0)(lambda: body(odd_ref, even_ref))
    pl.when(step % 2 == 1)(lambda: body(even_ref, odd_ref))


def _gla_fwd_out_step(io_refs, s_ref, work_refs, of_pend, of_new, *, t, nc, n_rows):
    (q_ref, k_ref, v_ref, lr_ref, wdec, bdec, sign_ref, ob_ref, gg_ref,
     nq_ref, nkp_ref, nkc_ref, nkn_ref, nvp_ref, nvc_ref, nvn_ref, ng_ref, bias_ref, mq_ref, mg_ref, mkv_ref,
     x_ref, w_ref, gng_ref, post_ref, y_ref) = io_refs
    kbuf, vbuf, nat_ref, mem_ref = work_refs

    chunk_rows = [slice(c * t, (c + 1) * t) for c in range(nc)]
    la = [_gla_log_decay(_Window(lr_ref, r, slice(0, LR_PAD)), wdec, bdec, reverse=False) for r in chunk_rows]

    acc = None
    for h in range(GLA_HEADS):
        lanes = slice(h * GLA_DV, (h + 1) * GLA_DV)
        o = of_pend[:, lanes].astype(F32) + ob_ref[:, lanes].astype(F32)
        ms = jnp.mean(o * o, axis=-1, keepdims=True)
        on = (o * lax.rsqrt(ms + EPS)) * gng_ref[...]
        og = (on * gg_ref[:, lanes].astype(F32)).astype(BF16)
        part = jnp.dot(og, w_ref[lanes, :], preferred_element_type=F32)
        acc = part if acc is None else acc + part

    b = [_gla_cumulate(la_c, reverse=False, t=t) for la_c in la]
    pre = [_gla_factors(b[c], _Window(q_ref, r, slice(0, GLA_QK)), _Window(k_ref, r, slice(0, GLA_QK)), sign_ref,
                        reverse=False, t=t)
           for c, r in enumerate(chunk_rows)]

    prev_block = jnp.maximum(pl.program_id(1) - 1, 0)
    _nat_fill(nkp_ref, nkc_ref, nkn_ref, nvp_ref, nvc_ref, nvn_ref, kbuf, vbuf)
    nat_rows = list(range(NAT_TILE_ROWS))
    rows_per_part = NAT_TILE_ROWS // (nc * GLA_HEADS)
    intra = {}
    waiting = None

    def finish(item):
        (c, h), att = item
        intra[c, h] = _gla_head_pv(att, _Window(v_ref, chunk_rows[c], slice(0, GLA_V)), h)

    for h in range(GLA_HEADS):
        for c in range(nc):
            for rr in nat_rows[:rows_per_part]:
                _nat_row(prev_block, rr, nq_ref, ng_ref, bias_ref, nat_ref, kbuf, vbuf, n_rows=n_rows)
            del nat_rows[:rows_per_part]
            att = _gla_head_levels(pre[c], h, reverse=False, t=t)
            if waiting is not None:
                finish(waiting)
            waiting = ((c, h), att)
    finish(waiting)
    for rr in nat_rows:
        _nat_row(prev_block, rr, nq_ref, ng_ref, bias_ref, nat_ref, kbuf, vbuf, n_rows=n_rows)
    _memattn_block(mq_ref, mg_ref, mkv_ref, mem_ref)

    acc = acc + jnp.dot(nat_ref[...], w_ref[GLA_V:GLA_V + NAT_W, :], preferred_element_type=F32)
    acc = acc + jnp.dot(mem_ref[...], w_ref[GLA_V + NAT_W:MIX_W, :], preferred_element_type=F32)
    ms = jnp.mean(acc * acc, axis=-1, keepdims=True)
    y_ref[...] = x_ref[...] + (acc * lax.rsqrt(ms + EPS)) * post_ref[...]

    for c, r in enumerate(chunk_rows):
        for h in range(GLA_HEADS):
            _gla_head_inter(pre[c], intra[c, h], _Window(of_new, r, slice(0, GLA_V)), s_ref, h)


def _gla_fwd_out(p, o_b, mkv, bias_tab, x2, wdec, bdec, w_out, gla_ng, post_g, n_seq, n_tok, t, nc):
    tb = t * nc
    assert tb == NAT_TILE, "the attention branches work on 8 grid rows per step"
    nt = n_tok // tb
    rows = n_seq * n_tok
    halo_per_blk = tb // NAT_HALO

    signs = jnp.asarray(_gla_role_signs(t, reverse=False))

    def gla_blk(col):
        return lambda s, i: (s * nt + jnp.minimum(i, nt - 1), col)

    def out_blk(col):
        return lambda s, i: (s * nt + jnp.maximum(i - 1, 0), col)

    def halo_blk(col, side):
        def index(s, i):
            j = jnp.maximum(i - 1, 0) * halo_per_blk
            j = jnp.maximum(j - 1, 0) if side < 0 else jnp.minimum(j + halo_per_blk, nt * halo_per_blk - 1)
            return (s * nt * halo_per_blk + j, col)
        return index

    nat_blk = (tb, NAT_W)
    halo = (NAT_HALO, NAT_W)
    nq_col, nk_col, nv_col, ng_col = (c // NAT_W for c in (P_NQ, P_NK, P_NV, P_NG))

    return pl.pallas_call(
        functools.partial(_gla_fwd_out_kernel, t=t, nc=nc, n_rows=n_tok // GRID_W),
        grid=(n_seq, nt + 1),
        in_specs=[
            pl.BlockSpec((tb, GLA_QK), gla_blk(P_GQ // GLA_QK)),
            pl.BlockSpec((tb, GLA_QK), gla_blk(P_GK // GLA_QK)),
            pl.BlockSpec((tb, GLA_V), gla_blk(P_GV // GLA_V)),
            pl.BlockSpec((tb, LR_PAD), gla_blk(P_LR // LR_PAD)),
            _resident((LR_PAD, 2 * GLA_QK), lambda s, i: (0, 0)),
            _resident((1, 2 * GLA_QK), lambda s, i: (0, 0)),
            _resident(signs.shape, lambda s, i: (0, 0, 0)),
            pl.BlockSpec((tb, GLA_V), out_blk(0)),
            pl.BlockSpec((tb, GLA_V), out_blk(P_GG // GLA_V)),
            pl.BlockSpec(nat_blk, out_blk(nq_col)),
            pl.BlockSpec(halo, halo_blk(nk_col, -1)), pl.BlockSpec(nat_blk, out_blk(nk_col)),
            pl.BlockSpec(halo, halo_blk(nk_col, 1)),
            pl.BlockSpec(halo, halo_blk(nv_col, -1)), pl.BlockSpec(nat_blk, out_blk(nv_col)),
            pl.BlockSpec(halo, halo_blk(nv_col, 1)),
            pl.BlockSpec(nat_blk, out_blk(ng_col)),
            _resident(bias_tab.shape, lambda s, i: (0, 0, 0, 0)),
            pl.BlockSpec((tb, MEM_W), out_blk(P_MQ // MEM_W)),
            pl.BlockSpec((tb, MEM_W), out_blk(P_MG // MEM_W)),
            pl.BlockSpec((N_MEM, MKV_W), lambda s, i: (s, 0)),
            pl.BlockSpec((tb, D_MODEL), out_blk(0)),
            _resident((MIX_W, D_MODEL), lambda s, i: (0, 0)),
            _resident((1, GLA_DV), lambda s, i: (0, 0)),
            _resident((1, D_MODEL), lambda s, i: (0, 0)),
        ],
        out_specs=pl.BlockSpec((tb, D_MODEL), out_blk(0)),
        out_shape=jax.ShapeDtypeStruct((rows, D_MODEL), F32),
        scratch_shapes=[pltpu.VMEM((GLA_HEADS, GLA_DV, GLA_DK), F32),
                        pltpu.VMEM((tb, GLA_V), BF16), pltpu.VMEM((tb, GLA_V), BF16),
                        pltpu.VMEM((NAT_BUF, NAT_W), BF16), pltpu.VMEM((NAT_BUF, 2 * NAT_W), BF16),
                        pltpu.VMEM((tb, NAT_W), BF16), pltpu.VMEM((tb, MEM_W), BF16)],
        compiler_params=_cparams(2),
        name="gla_fwd_out",
    )(p, p, p, p, wdec, bdec, signs, o_b, p, p, p, p, p, p, p, p, p, bias_tab, p, p, mkv, x2, w_out, gla_ng, post_g)


NAT_TILE_ROWS = 8
NAT_TILE = NAT_TILE_ROWS * GRID_W
NAT_KEYS = NAT_KH * GRID_W


NAT_BIAS_ROWS = 2 * NAT_KH - 2
PAIR_W = 2 * NAT_DH


def _nat_bias_table(rpb):
    cq = np.arange(GRID_W)[:, None]
    ck = np.arange(GRID_W)[None, :]
    cs = np.clip(cq - NAT_KW // 2, 0, GRID_W - NAT_KW)
    col_in = (ck >= cs) & (ck < cs + NAT_KW)
    dcol = np.clip(ck - cq + (NAT_KW - 1), 0, 2 * NAT_KW - 2)
    onehot = ((dcol[None] == np.arange(2 * NAT_KW - 1)[:, None, None]) & col_in[None]).astype(np.float32)
    tab = jnp.einsum("hrd,dqk->hrqk", rpb.astype(F32), jnp.asarray(onehot),
                     precision=lax.Precision.HIGHEST)
    tab = tab * LOG2_E + jnp.asarray(np.where(col_in, 0.0, NEG_BIG).astype(np.float32))
    tab = jnp.concatenate([tab[:, :-1], tab[:, 1:]], axis=-1)
    tab = tab.reshape(NAT_PAIRS, 2, NAT_BIAS_ROWS, GRID_W, PAIR_W)
    return tab.transpose(0, 2, 1, 3, 4).reshape(NAT_PAIRS, NAT_BIAS_ROWS, PAIR_W, PAIR_W)


NAT_HALO = (NAT_KH // 2) * GRID_W
NAT_BUF = NAT_TILE + 2 * NAT_HALO


def _nat_fill(kp_ref, kc_ref, kn_ref, vp_ref, vc_ref, vn_ref, kbuf, vbuf):
    kbuf[0:NAT_HALO, :] = kp_ref[...]
    kbuf[NAT_HALO:NAT_HALO + NAT_TILE, :] = kc_ref[...]
    kbuf[NAT_HALO + NAT_TILE:NAT_BUF, :] = kn_ref[...]
    for r0, r1, v_ref in ((0, NAT_HALO, vp_ref), (NAT_HALO, NAT_HALO + NAT_TILE, vc_ref),
                          (NAT_HALO + NAT_TILE, NAT_BUF, vn_ref)):
        ones = jnp.ones((r1 - r0, PAIR_W), BF16)
        for p in range(NAT_PAIRS):
            vbuf[r0:r1, 2 * p * PAIR_W:(2 * p + 1) * PAIR_W] = v_ref[:, p * PAIR_W:(p + 1) * PAIR_W]
            vbuf[r0:r1, (2 * p + 1) * PAIR_W:(2 * p + 2) * PAIR_W] = ones


def _nat_row(j, rr, q_ref, g_ref, bias_ref, o_ref, kbuf, vbuf, *, n_rows):
    lane = lax.broadcasted_iota(jnp.int32, (GRID_W, PAIR_W), 1)
    first = lane < NAT_DH
    row0 = j * NAT_TILE_ROWS
    r = row0 + rr
    rs = jnp.clip(r - NAT_KH // 2, 0, n_rows - NAT_KH)
    d0 = rs - r + (NAT_KH - 1)
    off = pl.multiple_of((rs - row0 + NAT_KH // 2) * GRID_W, GRID_W)
    rows = slice(rr * GRID_W, (rr + 1) * GRID_W)
    scores = []
    for p in range(NAT_PAIRS):
        lanes = slice(p * PAIR_W, (p + 1) * PAIR_W)
        qp = q_ref[rows, lanes]
        zero = jnp.zeros_like(qp)
        q2 = jnp.concatenate([jnp.where(first, qp, zero), jnp.where(first, zero, qp)], axis=0)
        kk = kbuf[pl.ds(off, NAT_KEYS), lanes]
        s = lax.dot_general(q2, kk, (((1,), (1,)), ((), ())), preferred_element_type=F32)
        bias = jnp.concatenate([bias_ref[p, d0 + 2 * jj] for jj in range(NAT_KH // 2)], axis=1)
        scores.append(s + bias)
    probs = []
    for s in scores:
        probs.append(jnp.exp2(s - jnp.max(s, axis=-1, keepdims=True)).astype(BF16))
    for p, e in enumerate(probs):
        lanes = slice(p * PAIR_W, (p + 1) * PAIR_W)
        vv = vbuf[pl.ds(off, NAT_KEYS), 2 * p * PAIR_W:(2 * p + 2) * PAIR_W]
        pv = jnp.dot(e, vv, preferred_element_type=F32)
        pv = pv[:, 0:PAIR_W] / pv[:, PAIR_W:2 * PAIR_W]
        o = jnp.where(first, pv[0:GRID_W, :], pv[GRID_W:2 * GRID_W, :])
        gate = g_ref[rows, lanes].astype(F32)
        o_ref[rows, lanes] = (o * gate).astype(o_ref.dtype)


MKV_W = MEM_W + 2 * MEM_W


def _memkv_kernel(m_ref, g_ref, w_ref, o_ref):
    x = m_ref[...]
    ms = jnp.mean(x * x, axis=-1, keepdims=True)
    h = ((x * lax.rsqrt(ms + EPS)) * g_ref[...]).astype(BF16)
    kv = jnp.dot(h, w_ref[...], preferred_element_type=F32).astype(o_ref.dtype)
    o_ref[:, 0:MEM_W] = kv[:, 0:MEM_W]
    ones = jnp.ones((N_MEM, MEM_DH), o_ref.dtype)
    for hd in range(MEM_HEADS):
        c0 = MEM_W + 2 * hd * MEM_DH
        o_ref[:, c0:c0 + MEM_DH] = kv[:, MEM_W + hd * MEM_DH:MEM_W + (hd + 1) * MEM_DH]
        o_ref[:, c0 + MEM_DH:c0 + 2 * MEM_DH] = ones


def _memkv(mem2, mem_g, w_kv):
    rows = mem2.shape[0]
    return pl.pallas_call(
        _memkv_kernel,
        grid=(rows // N_MEM,),
        in_specs=[
            pl.BlockSpec((N_MEM, D_MODEL), lambda i: (i, 0)),
            _resident((1, D_MODEL), lambda i: (0, 0)),
            _resident((D_MODEL, 2 * MEM_W), lambda i: (0, 0)),
        ],
        out_specs=pl.BlockSpec((N_MEM, MKV_W), lambda i: (i, 0)),
        out_shape=jax.ShapeDtypeStruct((rows, MKV_W), BF16),
        compiler_params=_cparams(1),
        name="memkv",
    )(mem2, mem_g, w_kv)


def _memattn_block(q_ref, g_ref, kv_ref, o_ref):
    for h in range(MEM_HEADS):
        lanes = slice(h * MEM_DH, (h + 1) * MEM_DH)
        kh = kv_ref[:, lanes]
        vh = kv_ref[:, MEM_W + 2 * h * MEM_DH:MEM_W + 2 * (h + 1) * MEM_DH]
        s = lax.dot_general(q_ref[:, lanes], kh, (((1,), (1,)), ((), ())), preferred_element_type=F32)
        e = jnp.exp2(s - jnp.max(s, axis=-1, keepdims=True)).astype(BF16)
        pv = jnp.dot(e, vh, preferred_element_type=F32)
        o = pv[:, 0:MEM_DH] / pv[:, MEM_DH:2 * MEM_DH]
        gate = g_ref[:, lanes].astype(F32)
        o_ref[:, lanes] = (o * gate).astype(o_ref.dtype)


def _choose_tile(n, pref):
    t = pref
    while n % t:
        t //= 2
    return t


def _prepare_weights(w_in, gw_f, gb_f, gw_b, gb_b, w_mem_kv, w_out):
    lr0 = 2 * GLA_QK + 2 * GLA_V
    lr1 = lr0 + 2 * GLA_RANK
    pad = jnp.zeros((D_MODEL, LR_PAD - 2 * GLA_RANK), w_in.dtype)
    w_in_p = jnp.concatenate([w_in[:, :lr0], w_in[:, lr1:], w_in[:, lr0:lr1], pad], axis=1).astype(BF16)
    zero = jnp.zeros((GLA_RANK, GLA_QK), gw_f.dtype)
    wdec = jnp.concatenate([jnp.concatenate([gw_f, zero], axis=1), jnp.concatenate([zero, gw_b], axis=1),
                            jnp.zeros((LR_PAD - 2 * GLA_RANK, 2 * GLA_QK), gw_f.dtype)], axis=0)
    bdec = jnp.concatenate([gb_f, gb_b]).reshape(1, 2 * GLA_QK).astype(F32)
    return w_in_p, wdec.astype(BF16), bdec, w_mem_kv.astype(BF16), w_out.astype(BF16)


def _trunk(x, mkv, pre_g, w_in_p, wdec, bdec, gla_ng, bias_tab, w_out, post_g):
    n_seq, n_tok, _ = x.shape
    rows = n_seq * n_tok
    x2 = x.reshape(rows, D_MODEL)
    t_gla = _choose_tile(n_tok, 256)
    n_chunks = 2 if n_tok % (2 * t_gla) == 0 else 1
    p, o_b = _proj_gla_bwd(x2, pre_g, w_in_p, wdec, bdec, n_seq, n_tok, t_gla, n_chunks)
    y = _gla_fwd_out(p, o_b, mkv, bias_tab, x2, wdec, bdec, w_out, gla_ng, post_g, n_seq, n_tok, t_gla, n_chunks)
    return y.reshape(x.shape)


def kernel(x_prompt, x_sample, mem_prompt, mem_sample, pre_norm_g, w_in, gla_w_fwd, gla_b_fwd, gla_w_bwd,
           gla_b_bwd, gla_norm_g, nat_rpb, mem_norm_g, w_mem_kv, w_out, post_norm_g):
    assert pre_norm_g.shape[0] == 1, "single-layer trunk"
    w_in_p, wdec, bdec, w_kv, w_o = _prepare_weights(
        w_in[0], gla_w_fwd[0], gla_b_fwd[0], gla_w_bwd[0], gla_b_bwd[0], w_mem_kv[0], w_out[0])
    pre_g = pre_norm_g[0].reshape(1, D_MODEL)
    post_g = post_norm_g[0].reshape(1, D_MODEL)
    gla_ng = gla_norm_g[0].reshape(1, GLA_DV)
    mem_g = mem_norm_g[0].reshape(1, D_MODEL)
    bias_tab = _nat_bias_table(nat_rpb[0])

    n_p = mem_prompt.shape[0]
    mem_all = jnp.concatenate([mem_prompt, mem_sample], axis=0).reshape(-1, D_MODEL)
    mkv = _memkv(mem_all, mem_g, w_kv)
    mkv_p, mkv_s = mkv[:n_p * N_MEM], mkv[n_p * N_MEM:]

    run = functools.partial(_trunk, pre_g=pre_g, w_in_p=w_in_p, wdec=wdec, bdec=bdec, gla_ng=gla_ng,
                            bias_tab=bias_tab, w_out=w_o, post_g=post_g)
    return (run(x_prompt, mkv_p), run(x_sample, mkv_s))
```

```python
import functools

import numpy as np
import jax
import jax.numpy as jnp
from jax import lax
from jax.experimental import pallas as pl
from jax.experimental.pallas import tpu as pltpu

F32 = jnp.float32
BF16 = jnp.bfloat16

D_MODEL = 1024
N_MEM = 256
GRID_W = 64
EPS = 1e-6

GLA_HEADS = 4
GLA_DK = 128
GLA_DV = 256
GLA_RANK = 16
GLA_TAU = 16.0
GLA_QK = GLA_HEADS * GLA_DK
GLA_V = GLA_HEADS * GLA_DV

NAT_HEADS = 8
NAT_DH = 64
NAT_KH = 8
NAT_KW = 16
NAT_W = NAT_HEADS * NAT_DH
NAT_PAIRS = NAT_HEADS // 2

MEM_HEADS = 4
MEM_DH = 128
MEM_W = MEM_HEADS * MEM_DH

MIX_W = GLA_V + NAT_W + MEM_W

COL_BLK = 512
P_GQ, P_GK, P_GV, P_GG = 0, 512, 1024, 2048
P_NQ, P_NK, P_NV, P_NG = 3072, 3584, 4096, 4608
P_MQ, P_MG = 5120, 5632
P_LR = 6144
LR_PAD = 128
P_W = P_LR + LR_PAD

NEG_BIG = -1e30

VMEM_LIMIT = 56 * 1024 * 1024


def _cparams(n_axes):
    return pltpu.CompilerParams(
        dimension_semantics=("arbitrary",) * n_axes,
        vmem_limit_bytes=VMEM_LIMIT,
    )


def _resident(shape, index_map):
    return pl.BlockSpec(shape, index_map, pipeline_mode=pl.Buffered(1))


def _silu(g):
    return g * jax.nn.sigmoid(g)


def _normed_input(x_ref, g_ref):
    x = x_ref[...]
    ms = jnp.mean(x * x, axis=-1, keepdims=True)
    return ((x * lax.rsqrt(ms + EPS)) * g_ref[...]).astype(BF16)


def _gla_input_col(c0):
    if c0 < P_GV + GLA_V:
        return c0
    return 2 * GLA_QK + GLA_V if c0 == P_LR else None


def _project_chunks(h, w_ref, o_ref, chunks, gla_ref=None):
    for c0, c1, post in chunks:
        acc = jnp.dot(h, w_ref[:, c0:c1], preferred_element_type=F32)
        if post == "gate":
            acc = _silu(acc)
        elif post is not None:
            acc = acc * post
        val = acc.astype(BF16)
        o_ref[:, c0:c1] = val
        g0 = _gla_input_col(c0)
        if gla_ref is not None and g0 is not None:
            gla_ref[:, g0:g0 + (c1 - c0)] = val


def _proj_scales():
    post = {P_GQ: GLA_DK ** -0.5, P_NQ: LOG2_E * NAT_DH ** -0.5, P_MQ: LOG2_E * MEM_DH ** -0.5,
            P_GG: "gate", P_GG + COL_BLK: "gate", P_NG: "gate", P_MG: "gate"}
    chunks = [(c0, c0 + COL_BLK, post.get(c0)) for c0 in range(0, P_LR, COL_BLK)]
    return tuple(chunks + [(P_LR, P_W, None)])


def _block_row_bcast(b, blk, ridx):
    t = b.shape[0]
    if blk >= 8:
        pieces = []
        for b0 in range(0, t, blk):
            row = b[b0 + ridx:b0 + ridx + 1, :]
            pieces.append(jnp.broadcast_to(row, (blk, b.shape[1])))
        return pieces[0] if len(pieces) == 1 else jnp.concatenate(pieces, axis=0)
    m = lax.broadcasted_iota(jnp.int32, b.shape, 0) & (blk - 1)
    out = b
    for pos in range(blk):
        if pos == ridx:
            continue
        shifted = pltpu.roll(b, (pos - ridx) % t, axis=0)
        out = jnp.where(m == pos, shifted, out)
    return out


class _Window:
    def __init__(self, ref, rows, cols):
        self.ref, self.rows, self.cols, self.dtype = ref, rows, cols, ref.dtype

    def _abs(self, idx):
        idx = (slice(None), slice(None)) if idx is Ellipsis else idx
        out = []
        for sl, base in zip(idx, (self.rows, self.cols)):
            lo = base.start + (sl.start or 0)
            hi = base.stop if sl.stop is None else base.start + sl.stop
            out.append(slice(lo, hi))
        return tuple(out)

    def __getitem__(self, idx):
        return self.ref[self._abs(idx)]

    def __setitem__(self, idx, value):
        self.ref[self._abs(idx)] = value


def _log_sigmoid(z):
    return jnp.minimum(z, 0.0) - jnp.log(1.0 + jnp.exp(-jnp.abs(z)))


LOG2_E = 1.4426950408889634


def _gla_log_decay(lr_ref, wdec_ref, bdec_ref, *, reverse):
    d = 1 if reverse else 0
    z = jnp.dot(lr_ref[...], wdec_ref[:, d * GLA_QK:(d + 1) * GLA_QK], preferred_element_type=F32)
    z = z + bdec_ref[:, d * GLA_QK:(d + 1) * GLA_QK]
    return _log_sigmoid(z) * (LOG2_E / GLA_TAU)


def _gla_cumulate(la, *, reverse, t):
    ri = lax.broadcasted_iota(jnp.int32, (t, t), 0)
    ci = lax.broadcasted_iota(jnp.int32, (t, t), 1)
    tri = jnp.where((ci >= ri) if reverse else (ci <= ri), 1.0, 0.0).astype(BF16)
    la_hi = la.astype(BF16)
    la_lo = (la - la_hi.astype(F32)).astype(BF16)
    b = (jnp.dot(tri, la_hi, preferred_element_type=F32)
         + jnp.dot(tri, la_lo, preferred_element_type=F32))
    return b


def _gla_role_signs(t, reverse):
    rows = np.arange(t)
    signs = []
    s = t // 2
    while s >= 1:
        later = (rows & (2 * s - 1)) >= s
        signs.append(np.where(later != reverse, 1.0, -1.0))
        s //= 2
    return np.ascontiguousarray(np.broadcast_to(np.stack(signs)[:, :, None], (len(signs), t, GLA_DK)),
                                dtype=np.float32)


def _gla_factors(b, q_ref, k_ref, sign_ref, *, reverse, t):
    q_all = q_ref[...]
    k_all = k_ref[...]
    b_last = b[0:1, :] if reverse else b[t - 1:t, :]

    levels = []
    s = t // 2
    while s >= 1:
        ridx = s if reverse else s - 1
        dist = b - _block_row_bcast(b, 2 * s, ridx)
        sign = sign_ref[len(levels)]
        neg = jnp.concatenate([dist[:, h * GLA_DK:(h + 1) * GLA_DK] * sign for h in range(GLA_HEADS)], axis=1)
        e_l = jnp.exp2(neg).astype(BF16)
        levels.append((s.bit_length() - 1, q_all * e_l, k_all * e_l))
        s //= 2
    return dict(
        q=q_all, k=k_all, levels=levels,
        q_in=q_all * jnp.exp2(b).astype(BF16),
        k_out=k_all * jnp.exp2(b_last - b).astype(BF16),
        e_all=jnp.exp2(b_last))


def _gla_head_levels(pre, h, *, reverse, t):
    half = t // 2
    rh = lax.broadcasted_iota(jnp.int32, (half, half), 0)
    ch = lax.broadcasted_iota(jnp.int32, (half, half), 1)
    xor_f = (rh ^ ch).astype(F32)
    level_of = lax.shift_right_logical(pltpu.bitcast(xor_f, jnp.int32), 23) - 127
    level_of = jnp.where((ch > rh) if reverse else (ch < rh), level_of, -1)

    nt_dims = (((1,), (1,)), ((), ()))
    early = slice(half, t) if reverse else slice(0, half)
    late = slice(0, half) if reverse else slice(half, t)
    kl = slice(h * GLA_DK, (h + 1) * GLA_DK)
    levels = pre["levels"]

    within = {}
    for rows in (early, late):
        qk = pre["q"][rows, kl].astype(F32) * pre["k"][rows, kl].astype(F32)
        att = jnp.where(rh == ch, jnp.sum(qk, axis=-1, keepdims=True), 0.0)
        for lvl, qt, kt in levels[1:]:
            p = lax.dot_general(qt[rows, kl], kt[rows, kl], nt_dims, preferred_element_type=F32)
            att = jnp.where(level_of == lvl, p, att)
        within[rows.start] = att.astype(BF16)
    _, qt, kt = levels[0]
    cross = lax.dot_general(qt[late, kl], kt[early, kl], nt_dims, preferred_element_type=F32).astype(BF16)
    blocks = [within[late.start], cross] if reverse else [cross, within[late.start]]
    return early, late, within[early.start], jnp.concatenate(blocks, axis=1)


def _gla_head_pv(att, v_ref, h):
    early, late, att_early, att_late = att
    vh = v_ref[:, h * GLA_DV:(h + 1) * GLA_DV]
    o_late = jnp.dot(att_late, vh, preferred_element_type=F32)
    o_early = jnp.dot(att_early, vh[early], preferred_element_type=F32)
    return ((early, o_early), (late, o_late)), vh


def _gla_head_inter(pre, intra, o_ref, s_ref, h):
    parts, vh = intra
    kl = slice(h * GLA_DK, (h + 1) * GLA_DK)
    vl = slice(h * GLA_DV, (h + 1) * GLA_DV)
    st = s_ref[h]
    st_b = st.astype(BF16)
    for rows, o in parts:
        o = o + lax.dot_general(pre["q_in"][rows, kl], st_b, (((1,), (1,)), ((), ())),
                                preferred_element_type=F32)
        o_ref[rows, vl] = o.astype(o_ref.dtype)

    upd = lax.dot_general(vh, pre["k_out"][:, kl], (((0,), (0,)), ((), ())), preferred_element_type=F32)
    s_ref[h] = st * pre["e_all"][:, kl] + upd


GLA_IN_W = 2 * GLA_QK + GLA_V + LR_PAD


def _proj_gla_bwd_kernel(x0_ref, xn_ref, g_ref, w_ref, wdec, bdec, sign_ref, p_ref, ob_ref, s_ref, even_ref, odd_ref,
                         h_even, h_odd, *, scales, t, nc):
    step = pl.program_id(1)

    @pl.when(step == 0)
    def _():
        s_ref[...] = jnp.zeros_like(s_ref)
        odd_ref[...] = jnp.zeros_like(odd_ref)
        h_odd[...] = _normed_input(x0_ref, g_ref)

    body = functools.partial(_proj_gla_bwd_step, xn_ref, g_ref, w_ref, wdec, bdec, sign_ref, p_ref, ob_ref, s_ref,
                             scales=scales, t=t, nc=nc)
    pl.when(step % 2 == 0)(lambda: body(odd_ref, even_ref, h_odd, h_even))
    pl.when(step % 2 == 1)(lambda: body(even_ref, odd_ref, h_even, h_odd))


def _proj_gla_bwd_step(xn_ref, g_ref, w_ref, wdec, bdec, sign_ref, p_ref, ob_ref, s_ref, pend_ref, next_ref,
                       h_ref, h_next_ref, *, scales, t, nc):
    order = list(reversed(range(nc)))
    rows_of = {c: slice(c * t, (c + 1) * t) for c in order}

    def pending(c, c0, c1):
        return _Window(pend_ref, rows_of[c], slice(c0, c1))

    la = {c: _gla_log_decay(pending(c, 2 * GLA_QK + GLA_V, GLA_IN_W), wdec, bdec, reverse=True) for c in order}
    hx = h_ref[...]
    n_slots = nc * GLA_HEADS
    per_slot = max(1, (len(scales) - 4) // n_slots)
    n_first = len(scales) - per_slot * n_slots
    n_cover = 2
    _project_chunks(hx, w_ref, p_ref, scales[:n_cover], next_ref)
    b = {c: _gla_cumulate(la[c], reverse=True, t=t) for c in order}
    _project_chunks(hx, w_ref, p_ref, scales[n_cover:n_first], next_ref)
    pre = {c: _gla_factors(b[c], pending(c, 0, GLA_QK), pending(c, GLA_QK, 2 * GLA_QK), sign_ref,
                           reverse=True, t=t) for c in order}
    intra = {}
    waiting = None
    for slot, (h, c) in enumerate((h, c) for h in range(GLA_HEADS) for c in order):
        _project_chunks(hx, w_ref, p_ref, scales[n_first + slot * per_slot:n_first + (slot + 1) * per_slot],
                        next_ref)
        att = _gla_head_levels(pre[c], h, reverse=True, t=t)
        if waiting is not None:
            intra[waiting[0]] = _gla_head_pv(waiting[1], pending(waiting[0][0], 2 * GLA_QK, 2 * GLA_QK + GLA_V),
                                             waiting[0][1])
        waiting = ((c, h), att)
    intra[waiting[0]] = _gla_head_pv(waiting[1], pending(waiting[0][0], 2 * GLA_QK, 2 * GLA_QK + GLA_V),
                                     waiting[0][1])
    for c in order:
        o_win = _Window(ob_ref, rows_of[c], slice(0, GLA_V))
        for h in range(GLA_HEADS):
            _gla_head_inter(pre[c], intra[c, h], o_win, s_ref, h)
    h_next_ref[...] = _normed_input(xn_ref, g_ref)


def _proj_gla_bwd(x2, pre_g, w_in_p, wdec, bdec, n_seq, n_tok, t, nc):
    tb = t * nc
    nt = n_tok // tb
    rows = n_seq * n_tok
    assert (P_GQ, P_GK, P_GV) == (0, GLA_QK, 2 * GLA_QK)
    signs = jnp.asarray(_gla_role_signs(t, reverse=True))

    def proj_blk(s, i):
        return (s * nt + nt - 1 - jnp.minimum(i, nt - 1), 0)

    def gla_blk(s, i):
        return (s * nt + nt - 1 - jnp.maximum(i - 1, 0), 0)

    return pl.pallas_call(
        functools.partial(_proj_gla_bwd_kernel, scales=_proj_scales(), t=t, nc=nc),
        grid=(n_seq, nt + 1),
        in_specs=[
            pl.BlockSpec((tb, D_MODEL), lambda s, i: (s * nt + nt - 1, 0), pipeline_mode=pl.Buffered(1)),
            pl.BlockSpec((tb, D_MODEL), lambda s, i: proj_blk(s, i + 1)),
            _resident((1, D_MODEL), lambda s, i: (0, 0)),
            _resident((D_MODEL, P_W), lambda s, i: (0, 0)),
            _resident((LR_PAD, 2 * GLA_QK), lambda s, i: (0, 0)),
            _resident((1, 2 * GLA_QK), lambda s, i: (0, 0)),
            _resident(signs.shape, lambda s, i: (0, 0, 0)),
        ],
        out_specs=[pl.BlockSpec((tb, P_W), proj_blk), pl.BlockSpec((tb, GLA_V), gla_blk)],
        out_shape=[jax.ShapeDtypeStruct((rows, P_W), BF16), jax.ShapeDtypeStruct((rows, GLA_V), BF16)],
        scratch_shapes=[pltpu.VMEM((GLA_HEADS, GLA_DV, GLA_DK), F32),
                        pltpu.VMEM((tb, GLA_IN_W), BF16), pltpu.VMEM((tb, GLA_IN_W), BF16),
                        pltpu.VMEM((tb, D_MODEL), BF16), pltpu.VMEM((tb, D_MODEL), BF16)],
        compiler_params=_cparams(2),
        name="proj_gla_bwd",
    )(x2, x2, pre_g, w_in_p, wdec, bdec, signs)


def _gla_fwd_out_kernel(*refs, t, nc, n_rows):
    s_ref, even_ref, odd_ref = refs[-7:-4]
    step = pl.program_id(1)

    @pl.when(step == 0)
    def _():
        s_ref[...] = jnp.zeros_like(s_ref)
        odd_ref[...] = jnp.zeros_like(odd_ref)

    body = functools.partial(_gla_fwd_out_step, refs[:-7], s_ref, refs[-4:], t=t, nc=nc, n_rows=n_rows)
    pl.when(step % 2 == 0)(lambda: body(odd_ref, even_ref))
    pl.when(step % 2 == 1)(lambda: body(even_ref, odd_ref))


def _gla_fwd_out_step(io_refs, s_ref, work_refs, of_pend, of_new, *, t, nc, n_rows):
    (q_ref, k_ref, v_ref, lr_ref, wdec, bdec, sign_ref, ob_ref, gg_ref,
     nq_ref, nkp_ref, nkc_ref, nkn_ref, nvp_ref, nvc_ref, nvn_ref, ng_ref, bias_ref, mq_ref, mg_ref, mkv_ref,
     x_ref, w_ref, gng_ref, post_ref, y_ref) = io_refs
    kbuf, vbuf, nat_ref, mem_ref = work_refs

    chunk_rows = [slice(c * t, (c + 1) * t) for c in range(nc)]
    la = [_gla_log_decay(_Window(lr_ref, r, slice(0, LR_PAD)), wdec, bdec, reverse=False) for r in chunk_rows]

    acc = None
    for h in range(GLA_HEADS):
        lanes = slice(h * GLA_DV, (h + 1) * GLA_DV)
        o = of_pend[:, lanes].astype(F32) + ob_ref[:, lanes].astype(F32)
        ms = jnp.mean(o * o, axis=-1, keepdims=True)
        on = (o * lax.rsqrt(ms + EPS)) * gng_ref[...]
        og = (on * gg_ref[:, lanes].astype(F32)).astype(BF16)
        part = jnp.dot(og, w_ref[lanes, :], preferred_element_type=F32)
        acc = part if acc is None else acc + part

    b = [_gla_cumulate(la_c, reverse=False, t=t) for la_c in la]
    pre = [_gla_factors(b[c], _Window(q_ref, r, slice(0, GLA_QK)), _Window(k_ref, r, slice(0, GLA_QK)), sign_ref,
                        reverse=False, t=t)
           for c, r in enumerate(chunk_rows)]

    prev_block = jnp.maximum(pl.program_id(1) - 1, 0)
    _nat_fill(nkp_ref, nkc_ref, nkn_ref, nvp_ref, nvc_ref, nvn_ref, kbuf, vbuf)
    nat_rows = list(range(NAT_TILE_ROWS))
    rows_per_part = NAT_TILE_ROWS // (nc * GLA_HEADS)
    intra = {}
    waiting = None

    def finish(item):
        (c, h), att = item
        intra[c, h] = _gla_head_pv(att, _Window(v_ref, chunk_rows[c], slice(0, GLA_V)), h)

    for h in range(GLA_HEADS):
        for c in range(nc):
            for rr in nat_rows[:rows_per_part]:
                _nat_row(prev_block, rr, nq_ref, ng_ref, bias_ref, nat_ref, kbuf, vbuf, n_rows=n_rows)
            del nat_rows[:rows_per_part]
            att = _gla_head_levels(pre[c], h, reverse=False, t=t)
            if waiting is not None:
                finish(waiting)
            waiting = ((c, h), att)
    finish(waiting)
    for rr in nat_rows:
        _nat_row(prev_block, rr, nq_ref, ng_ref, bias_ref, nat_ref, kbuf, vbuf, n_rows=n_rows)
    _memattn_block(mq_ref, mg_ref, mkv_ref, mem_ref)

    acc = acc + jnp.dot(nat_ref[...], w_ref[GLA_V:GLA_V + NAT_W, :], preferred_element_type=F32)
    acc = acc + jnp.dot(mem_ref[...], w_ref[GLA_V + NAT_W:MIX_W, :], preferred_element_type=F32)
    ms = jnp.mean(acc * acc, axis=-1, keepdims=True)
    y_ref[...] = x_ref[...] + (acc * lax.rsqrt(ms + EPS)) * post_ref[...]

    for c, r in enumerate(chunk_rows):
        for h in range(GLA_HEADS):
            _gla_head_inter(pre[c], intra[c, h], _Window(of_new, r, slice(0, GLA_V)), s_ref, h)


def _gla_fwd_out(p, o_b, mkv, bias_tab, x2, wdec, bdec, w_out, gla_ng, post_g, n_seq, n_tok, t, nc):
    tb = t * nc
    assert tb == NAT_TILE, "the attention branches work on 8 grid rows per step"
    nt = n_tok // tb
    rows = n_seq * n_tok
    halo_per_blk = tb // NAT_HALO

    signs = jnp.asarray(_gla_role_signs(t, reverse=False))

    def gla_blk(col):
        return lambda s, i: (s * nt + jnp.minimum(i, nt - 1), col)

    def out_blk(col):
        return lambda s, i: (s * nt + jnp.maximum(i - 1, 0), col)

    def halo_blk(col, side):
        def index(s, i):
            j = jnp.maximum(i - 1, 0) * halo_per_blk
            j = jnp.maximum(j - 1, 0) if side < 0 else jnp.minimum(j + halo_per_blk, nt * halo_per_blk - 1)
            return (s * nt * halo_per_blk + j, col)
        return index

    nat_blk = (tb, NAT_W)
    halo = (NAT_HALO, NAT_W)
    nq_col, nk_col, nv_col, ng_col = (c // NAT_W for c in (P_NQ, P_NK, P_NV, P_NG))

    return pl.pallas_call(
        functools.partial(_gla_fwd_out_kernel, t=t, nc=nc, n_rows=n_tok // GRID_W),
        grid=(n_seq, nt + 1),
        in_specs=[
            pl.BlockSpec((tb, GLA_QK), gla_blk(P_GQ // GLA_QK)),
            pl.BlockSpec((tb, GLA_QK), gla_blk(P_GK // GLA_QK)),
            pl.BlockSpec((tb, GLA_V), gla_blk(P_GV // GLA_V)),
            pl.BlockSpec((tb, LR_PAD), gla_blk(P_LR // LR_PAD)),
            _resident((LR_PAD, 2 * GLA_QK), lambda s, i: (0, 0)),
            _resident((1, 2 * GLA_QK), lambda s, i: (0, 0)),
            _resident(signs.shape, lambda s, i: (0, 0, 0)),
            pl.BlockSpec((tb, GLA_V), out_blk(0)),
            pl.BlockSpec((tb, GLA_V), out_blk(P_GG // GLA_V)),
            pl.BlockSpec(nat_blk, out_blk(nq_col)),
            pl.BlockSpec(halo, halo_blk(nk_col, -1)), pl.BlockSpec(nat_blk, out_blk(nk_col)),
            pl.BlockSpec(halo, halo_blk(nk_col, 1)),
            pl.BlockSpec(halo, halo_blk(nv_col, -1)), pl.BlockSpec(nat_blk, out_blk(nv_col)),
            pl.BlockSpec(halo, halo_blk(nv_col, 1)),
            pl.BlockSpec(nat_blk, out_blk(ng_col)),
            _resident(bias_tab.shape, lambda s, i: (0, 0, 0, 0)),
            pl.BlockSpec((tb, MEM_W), out_blk(P_MQ // MEM_W)),
            pl.BlockSpec((tb, MEM_W), out_blk(P_MG // MEM_W)),
            pl.BlockSpec((N_MEM, MKV_W), lambda s, i: (s, 0)),
            pl.BlockSpec((tb, D_MODEL), out_blk(0)),
            _resident((MIX_W, D_MODEL), lambda s, i: (0, 0)),
            _resident((1, GLA_DV), lambda s, i: (0, 0)),
            _resident((1, D_MODEL), lambda s, i: (0, 0)),
        ],
        out_specs=pl.BlockSpec((tb, D_MODEL), out_blk(0)),
        out_shape=jax.ShapeDtypeStruct((rows, D_MODEL), F32),
        scratch_shapes=[pltpu.VMEM((GLA_HEADS, GLA_DV, GLA_DK), F32),
                        pltpu.VMEM((tb, GLA_V), BF16), pltpu.VMEM((tb, GLA_V), BF16),
                        pltpu.VMEM((NAT_BUF, NAT_W), BF16), pltpu.VMEM((NAT_BUF, 2 * NAT_W), BF16),
                        pltpu.VMEM((tb, NAT_W), BF16), pltpu.VMEM((tb, MEM_W), BF16)],
        compiler_params=_cparams(2),
        name="gla_fwd_out",
    )(p, p, p, p, wdec, bdec, signs, o_b, p, p, p, p, p, p, p, p, p, bias_tab, p, p, mkv, x2, w_out, gla_ng, post_g)


NAT_TILE_ROWS = 8
NAT_TILE = NAT_TILE_ROWS * GRID_W
NAT_KEYS = NAT_KH * GRID_W


NAT_BIAS_ROWS = 2 * NAT_KH - 2
PAIR_W = 2 * NAT_DH


def _nat_bias_table(rpb):
    cq = np.arange(GRID_W)[:, None]
    ck = np.arange(GRID_W)[None, :]
    cs = np.clip(cq - NAT_KW // 2, 0, GRID_W - NAT_KW)
    col_in = (ck >= cs) & (ck < cs + NAT_KW)
    dcol = np.clip(ck - cq + (NAT_KW - 1), 0, 2 * NAT_KW - 2)
    onehot = ((dcol[None] == np.arange(2 * NAT_KW - 1)[:, None, None]) & col_in[None]).astype(np.float32)
    tab = jnp.einsum("hrd,dqk->hrqk", rpb.astype(F32), jnp.asarray(onehot),
                     precision=lax.Precision.HIGHEST)
    tab = tab * LOG2_E + jnp.asarray(np.where(col_in, 0.0, NEG_BIG).astype(np.float32))
    tab = jnp.concatenate([tab[:, :-1], tab[:, 1:]], axis=-1)
    tab = tab.reshape(NAT_PAIRS, 2, NAT_BIAS_ROWS, GRID_W, PAIR_W)
    return tab.transpose(0, 2, 1, 3, 4).reshape(NAT_PAIRS, NAT_BIAS_ROWS, PAIR_W, PAIR_W)


NAT_HALO = (NAT_KH // 2) * GRID_W
NAT_BUF = NAT_TILE + 2 * NAT_HALO


def _nat_fill(kp_ref, kc_ref, kn_ref, vp_ref, vc_ref, vn_ref, kbuf, vbuf):
    kbuf[0:NAT_HALO, :] = kp_ref[...]
    kbuf[NAT_HALO:NAT_HALO + NAT_TILE, :] = kc_ref[...]
    kbuf[NAT_HALO + NAT_TILE:NAT_BUF, :] = kn_ref[...]
    for r0, r1, v_ref in ((0, NAT_HALO, vp_ref), (NAT_HALO, NAT_HALO + NAT_TILE, vc_ref),
                          (NAT_HALO + NAT_TILE, NAT_BUF, vn_ref)):
        ones = jnp.ones((r1 - r0, PAIR_W), BF16)
        for p in range(NAT_PAIRS):
            vbuf[r0:r1, 2 * p * PAIR_W:(2 * p + 1) * PAIR_W] = v_ref[:, p * PAIR_W:(p + 1) * PAIR_W]
            vbuf[r0:r1, (2 * p + 1) * PAIR_W:(2 * p + 2) * PAIR_W] = ones


def _nat_row(j, rr, q_ref, g_ref, bias_ref, o_ref, kbuf, vbuf, *, n_rows):
    lane = lax.broadcasted_iota(jnp.int32, (GRID_W, PAIR_W), 1)
    first = lane < NAT_DH
    row0 = j * NAT_TILE_ROWS
    r = row0 + rr
    rs = jnp.clip(r - NAT_KH // 2, 0, n_rows - NAT_KH)
    d0 = rs - r + (NAT_KH - 1)
    off = pl.multiple_of((rs - row0 + NAT_KH // 2) * GRID_W, GRID_W)
    rows = slice(rr * GRID_W, (rr + 1) * GRID_W)
    scores = []
    for p in range(NAT_PAIRS):
        lanes = slice(p * PAIR_W, (p + 1) * PAIR_W)
        qp = q_ref[rows, lanes]
        zero = jnp.zeros_like(qp)
        q2 = jnp.concatenate([jnp.where(first, qp, zero), jnp.where(first, zero, qp)], axis=0)
        kk = kbuf[pl.ds(off, NAT_KEYS), lanes]
        s = lax.dot_general(q2, kk, (((1,), (1,)), ((), ())), preferred_element_type=F32)
        bias = jnp.concatenate([bias_ref[p, d0 + 2 * jj] for jj in range(NAT_KH // 2)], axis=1)
        scores.append(s + bias)
    probs = []
    for s in scores:
        probs.append(jnp.exp2(s - jnp.max(s, axis=-1, keepdims=True)).astype(BF16))
    for p, e in enumerate(probs):
        lanes = slice(p * PAIR_W, (p + 1) * PAIR_W)
        vv = vbuf[pl.ds(off, NAT_KEYS), 2 * p * PAIR_W:(2 * p + 2) * PAIR_W]
        pv = jnp.dot(e, vv, preferred_element_type=F32)
        pv = pv[:, 0:PAIR_W] / pv[:, PAIR_W:2 * PAIR_W]
        o = jnp.where(first, pv[0:GRID_W, :], pv[GRID_W:2 * GRID_W, :])
        gate = g_ref[rows, lanes].astype(F32)
        o_ref[rows, lanes] = (o * gate).astype(o_ref.dtype)


MKV_W = MEM_W + 2 * MEM_W


def _memkv_kernel(m_ref, g_ref, w_ref, o_ref):
    x = m_ref[...]
    ms = jnp.mean(x * x, axis=-1, keepdims=True)
    h = ((x * lax.rsqrt(ms + EPS)) * g_ref[...]).astype(BF16)
    kv = jnp.dot(h, w_ref[...], preferred_element_type=F32).astype(o_ref.dtype)
    o_ref[:, 0:MEM_W] = kv[:, 0:MEM_W]
    ones = jnp.ones((N_MEM, MEM_DH), o_ref.dtype)
    for hd in range(MEM_HEADS):
        c0 = MEM_W + 2 * hd * MEM_DH
        o_ref[:, c0:c0 + MEM_DH] = kv[:, MEM_W + hd * MEM_DH:MEM_W + (hd + 1) * MEM_DH]
        o_ref[:, c0 + MEM_DH:c0 + 2 * MEM_DH] = ones


def _memkv(mem2, mem_g, w_kv):
    rows = mem2.shape[0]
    return pl.pallas_call(
        _memkv_kernel,
        grid=(rows // N_MEM,),
        in_specs=[
            pl.BlockSpec((N_MEM, D_MODEL), lambda i: (i, 0)),
            _resident((1, D_MODEL), lambda i: (0, 0)),
            _resident((D_MODEL, 2 * MEM_W), lambda i: (0, 0)),
        ],
        out_specs=pl.BlockSpec((N_MEM, MKV_W), lambda i: (i, 0)),
        out_shape=jax.ShapeDtypeStruct((rows, MKV_W), BF16),
        compiler_params=_cparams(1),
        name="memkv",
    )(mem2, mem_g, w_kv)


def _memattn_block(q_ref, g_ref, kv_ref, o_ref):
    for h in range(MEM_HEADS):
        lanes = slice(h * MEM_DH, (h + 1) * MEM_DH)
        kh = kv_ref[:, lanes]
        vh = kv_ref[:, MEM_W + 2 * h * MEM_DH:MEM_W + 2 * (h + 1) * MEM_DH]
        s = lax.dot_general(q_ref[:, lanes], kh, (((1,), (1,)), ((), ())), preferred_element_type=F32)
        e = jnp.exp2(s - jnp.max(s, axis=-1, keepdims=True)).astype(BF16)
        pv = jnp.dot(e, vh, preferred_element_type=F32)
        o = pv[:, 0:MEM_DH] / pv[:, MEM_DH:2 * MEM_DH]
        gate = g_ref[:, lanes].astype(F32)
        o_ref[:, lanes] = (o * gate).astype(o_ref.dtype)


def _choose_tile(n, pref):
    t = pref
    while n % t:
        t //= 2
    return t


IN_W = 2 * GLA_QK + 2 * GLA_V + 2 * GLA_RANK + 4 * NAT_W + 2 * MEM_W
LR0 = 2 * GLA_QK + 2 * GLA_V
REGROUP_ROWS = 128


def _regroup_kernel(w_ref, o_ref):
    w = w_ref[0]
    lr1 = LR0 + 2 * GLA_RANK
    o_ref[:, 0:LR0] = w[:, 0:LR0].astype(BF16)
    o_ref[:, LR0:P_LR] = w[:, lr1:IN_W].astype(BF16)
    tail = jnp.concatenate([w[:, LR0:lr1], jnp.zeros((w.shape[0], LR_PAD - 2 * GLA_RANK), w.dtype)], axis=1)
    o_ref[:, P_LR:P_W] = tail.astype(BF16)


def _regroup_w_in(w_in):
    assert w_in.shape == (1, D_MODEL, IN_W) and P_LR == IN_W - 2 * GLA_RANK
    return pl.pallas_call(
        _regroup_kernel,
        grid=(D_MODEL // REGROUP_ROWS,),
        in_specs=[pl.BlockSpec((1, REGROUP_ROWS, IN_W), lambda i: (0, i, 0))],
        out_specs=pl.BlockSpec((REGROUP_ROWS, P_W), lambda i: (i, 0)),
        out_shape=jax.ShapeDtypeStruct((D_MODEL, P_W), BF16),
        compiler_params=_cparams(1),
        name="regroup_w_in",
    )(w_in)


def _prepare_weights(w_in, gw_f, gb_f, gw_b, gb_b, w_mem_kv, w_out):
    w_in_p = _regroup_w_in(w_in)
    zero = jnp.zeros((GLA_RANK, GLA_QK), gw_f.dtype)
    wdec = jnp.concatenate([jnp.concatenate([gw_f, zero], axis=1), jnp.concatenate([zero, gw_b], axis=1),
                            jnp.zeros((LR_PAD - 2 * GLA_RANK, 2 * GLA_QK), gw_f.dtype)], axis=0)
    bdec = jnp.concatenate([gb_f, gb_b]).reshape(1, 2 * GLA_QK).astype(F32)
    return w_in_p, wdec.astype(BF16), bdec, w_mem_kv.astype(BF16), w_out.astype(BF16)


def _trunk(x, mkv, pre_g, w_in_p, wdec, bdec, gla_ng, bias_tab, w_out, post_g):
    n_seq, n_tok, _ = x.shape
    rows = n_seq * n_tok
    x2 = x.reshape(rows, D_MODEL)
    t_gla = _choose_tile(n_tok, 256)
    n_chunks = 2 if n_tok % (2 * t_gla) == 0 else 1
    p, o_b = _proj_gla_bwd(x2, pre_g, w_in_p, wdec, bdec, n_seq, n_tok, t_gla, n_chunks)
    y = _gla_fwd_out(p, o_b, mkv, bias_tab, x2, wdec, bdec, w_out, gla_ng, post_g, n_seq, n_tok, t_gla, n_chunks)
    return y.reshape(x.shape)


def kernel(x_prompt, x_sample, mem_prompt, mem_sample, pre_norm_g, w_in, gla_w_fwd, gla_b_fwd, gla_w_bwd,
           gla_b_bwd, gla_norm_g, nat_rpb, mem_norm_g, w_mem_kv, w_out, post_norm_g):
    assert pre_norm_g.shape[0] == 1, "single-layer trunk"
    w_in_p, wdec, bdec, w_kv, w_o = _prepare_weights(
        w_in, gla_w_fwd[0], gla_b_fwd[0], gla_w_bwd[0], gla_b_bwd[0], w_mem_kv[0], w_out[0])
    pre_g = pre_norm_g[0].reshape(1, D_MODEL)
    post_g = post_norm_g[0].reshape(1, D_MODEL)
    gla_ng = gla_norm_g[0].reshape(1, GLA_DV)
    mem_g = mem_norm_g[0].reshape(1, D_MODEL)
    bias_tab = _nat_bias_table(nat_rpb[0])

    n_p = mem_prompt.shape[0]
    mem_all = jnp.concatenate([mem_prompt, mem_sample], axis=0).reshape(-1, D_MODEL)
    mkv = _memkv(mem_all, mem_g, w_kv)
    mkv_p, mkv_s = mkv[:n_p * N_MEM], mkv[n_p * N_MEM:]

    run = functools.partial(_trunk, pre_g=pre_g, w_in_p=w_in_p, wdec=wdec, bdec=bdec, gla_ng=gla_ng,
                            bias_tab=bias_tab, w_out=w_o, post_g=post_g)
    return (run(x_prompt, mkv_p), run(x_sample, mkv_s))
```

```python
import functools

import numpy as np
import jax
import jax.numpy as jnp
from jax import lax
from jax.experimental import pallas as pl
from jax.experimental.pallas import tpu as pltpu

F32 = jnp.float32
BF16 = jnp.bfloat16

D_MODEL = 1024
N_MEM = 256
GRID_W = 64
EPS = 1e-6

GLA_HEADS = 4
GLA_DK = 128
GLA_DV = 256
GLA_RANK = 16
GLA_TAU = 16.0
GLA_QK = GLA_HEADS * GLA_DK
GLA_V = GLA_HEADS * GLA_DV

NAT_HEADS = 8
NAT_DH = 64
NAT_KH = 8
NAT_KW = 16
NAT_W = NAT_HEADS * NAT_DH
NAT_PAIRS = NAT_HEADS // 2

MEM_HEADS = 4
MEM_DH = 128
MEM_W = MEM_HEADS * MEM_DH

MIX_W = GLA_V + NAT_W + MEM_W

COL_BLK = 512
P_GQ, P_GK, P_GV, P_GG = 0, 512, 1024, 2048
P_NQ, P_NK, P_NV, P_NG = 3072, 3584, 4096, 4608
P_MQ, P_MG = 5120, 5632
P_LR = 6144
LR_PAD = 128
P_W = P_LR + LR_PAD

NEG_BIG = -1e30

VMEM_LIMIT = 56 * 1024 * 1024


def _cparams(n_axes):
    return pltpu.CompilerParams(
        dimension_semantics=("arbitrary",) * n_axes,
        vmem_limit_bytes=VMEM_LIMIT,
    )


def _resident(shape, index_map):
    return pl.BlockSpec(shape, index_map, pipeline_mode=pl.Buffered(1))


def _silu(g):
    return g * jax.nn.sigmoid(g)


def _normed_input(x_ref, g_ref):
    x = x_ref[...]
    ms = jnp.mean(x * x, axis=-1, keepdims=True)
    return ((x * lax.rsqrt(ms + EPS)) * g_ref[...]).astype(BF16)


def _gla_input_col(c0):
    if c0 < P_GV + GLA_V:
        return c0
    return 2 * GLA_QK + GLA_V if c0 == P_LR else None


def _project_chunks(h, w_ref, o_ref, chunks, gla_ref=None):
    for c0, c1, post in chunks:
        acc = jnp.dot(h, w_ref[:, c0:c1], preferred_element_type=F32)
        if post == "gate":
            acc = _silu(acc)
        elif post is not None:
            acc = acc * post
        val = acc.astype(BF16)
        o_ref[:, c0:c1] = val
        g0 = _gla_input_col(c0)
        if gla_ref is not None and g0 is not None:
            gla_ref[:, g0:g0 + (c1 - c0)] = val


def _proj_scales():
    post = {P_GQ: GLA_DK ** -0.5, P_NQ: LOG2_E * NAT_DH ** -0.5, P_MQ: LOG2_E * MEM_DH ** -0.5,
            P_GG: "gate", P_GG + COL_BLK: "gate", P_NG: "gate", P_MG: "gate"}
    chunks = [(c0, c0 + COL_BLK, post.get(c0)) for c0 in range(0, P_LR, COL_BLK)]
    return tuple(chunks + [(P_LR, P_W, None)])


def _block_row_bcast(b, blk, ridx):
    t = b.shape[0]
    if blk >= 8:
        pieces = []
        for b0 in range(0, t, blk):
            row = b[b0 + ridx:b0 + ridx + 1, :]
            pieces.append(jnp.broadcast_to(row, (blk, b.shape[1])))
        return pieces[0] if len(pieces) == 1 else jnp.concatenate(pieces, axis=0)
    m = lax.broadcasted_iota(jnp.int32, b.shape, 0) & (blk - 1)
    out = b
    for pos in range(blk):
        if pos == ridx:
            continue
        shifted = pltpu.roll(b, (pos - ridx) % t, axis=0)
        out = jnp.where(m == pos, shifted, out)
    return out


class _Window:
    def __init__(self, ref, rows, cols):
        self.ref, self.rows, self.cols, self.dtype = ref, rows, cols, ref.dtype

    def _abs(self, idx):
        idx = (slice(None), slice(None)) if idx is Ellipsis else idx
        out = []
        for sl, base in zip(idx, (self.rows, self.cols)):
            lo = base.start + (sl.start or 0)
            hi = base.stop if sl.stop is None else base.start + sl.stop
            out.append(slice(lo, hi))
        return tuple(out)

    def __getitem__(self, idx):
        return self.ref[self._abs(idx)]

    def __setitem__(self, idx, value):
        self.ref[self._abs(idx)] = value


def _log_sigmoid(z):
    return jnp.minimum(z, 0.0) - jnp.log(1.0 + jnp.exp(-jnp.abs(z)))


LOG2_E = 1.4426950408889634


def _gla_log_decay(lr_ref, wdec_ref, bdec_ref, *, reverse):
    d = 1 if reverse else 0
    z = jnp.dot(lr_ref[...], wdec_ref[:, d * GLA_QK:(d + 1) * GLA_QK], preferred_element_type=F32)
    z = z + bdec_ref[:, d * GLA_QK:(d + 1) * GLA_QK]
    return _log_sigmoid(z) * (LOG2_E / GLA_TAU)


def _gla_cumulate(la, *, reverse, t):
    ri = lax.broadcasted_iota(jnp.int32, (t, t), 0)
    ci = lax.broadcasted_iota(jnp.int32, (t, t), 1)
    tri = jnp.where((ci >= ri) if reverse else (ci <= ri), 1.0, 0.0).astype(BF16)
    return jnp.dot(tri, la.astype(BF16), preferred_element_type=F32)


def _gla_role_signs(t, reverse):
    rows = np.arange(t)
    signs = []
    s = t // 2
    while s >= 1:
        later = (rows & (2 * s - 1)) >= s
        signs.append(np.where(later != reverse, 1.0, -1.0))
        s //= 2
    return np.ascontiguousarray(np.broadcast_to(np.stack(signs)[:, :, None], (len(signs), t, GLA_DK)),
                                dtype=np.float32)


def _gla_factors(b, q_ref, k_ref, sign_ref, *, reverse, t):
    q_all = q_ref[...]
    k_all = k_ref[...]
    b_last = b[0:1, :] if reverse else b[t - 1:t, :]

    levels = []
    s = t // 2
    while s >= 1:
        ridx = s if reverse else s - 1
        dist = b - _block_row_bcast(b, 2 * s, ridx)
        sign = sign_ref[len(levels)]
        neg = jnp.concatenate([dist[:, h * GLA_DK:(h + 1) * GLA_DK] * sign for h in range(GLA_HEADS)], axis=1)
        e_l = jnp.exp2(neg).astype(BF16)
        levels.append((s.bit_length() - 1, q_all * e_l, k_all * e_l))
        s //= 2
    return dict(
        q=q_all, k=k_all, levels=levels,
        q_in=q_all * jnp.exp2(b).astype(BF16),
        k_out=k_all * jnp.exp2(b_last - b).astype(BF16),
        e_all=jnp.exp2(b_last))


def _gla_head_levels(pre, h, *, reverse, t):
    half = t // 2
    rh = lax.broadcasted_iota(jnp.int32, (half, half), 0)
    ch = lax.broadcasted_iota(jnp.int32, (half, half), 1)
    xor_f = (rh ^ ch).astype(F32)
    level_of = lax.shift_right_logical(pltpu.bitcast(xor_f, jnp.int32), 23) - 127
    level_of = jnp.where((ch > rh) if reverse else (ch < rh), level_of, -1)

    nt_dims = (((1,), (1,)), ((), ()))
    early = slice(half, t) if reverse else slice(0, half)
    late = slice(0, half) if reverse else slice(half, t)
    kl = slice(h * GLA_DK, (h + 1) * GLA_DK)
    levels = pre["levels"]

    within = {}
    for rows in (early, late):
        qk = pre["q"][rows, kl].astype(F32) * pre["k"][rows, kl].astype(F32)
        att = jnp.where(rh == ch, jnp.sum(qk, axis=-1, keepdims=True), 0.0)
        for lvl, qt, kt in levels[1:]:
            p = lax.dot_general(qt[rows, kl], kt[rows, kl], nt_dims, preferred_element_type=F32)
            att = jnp.where(level_of == lvl, p, att)
        within[rows.start] = att.astype(BF16)
    _, qt, kt = levels[0]
    cross = lax.dot_general(qt[late, kl], kt[early, kl], nt_dims, preferred_element_type=F32).astype(BF16)
    blocks = [within[late.start], cross] if reverse else [cross, within[late.start]]
    return early, late, within[early.start], jnp.concatenate(blocks, axis=1)


def _gla_head_pv(att, v_ref, h):
    early, late, att_early, att_late = att
    vh = v_ref[:, h * GLA_DV:(h + 1) * GLA_DV]
    o_late = jnp.dot(att_late, vh, preferred_element_type=F32)
    o_early = jnp.dot(att_early, vh[early], preferred_element_type=F32)
    return ((early, o_early), (late, o_late)), vh


def _gla_head_inter(pre, intra, o_ref, s_ref, h):
    parts, vh = intra
    kl = slice(h * GLA_DK, (h + 1) * GLA_DK)
    vl = slice(h * GLA_DV, (h + 1) * GLA_DV)
    st = s_ref[h]
    st_b = st.astype(BF16)
    for rows, o in parts:
        o = o + lax.dot_general(pre["q_in"][rows, kl], st_b, (((1,), (1,)), ((), ())),
                                preferred_element_type=F32)
        o_ref[rows, vl] = o.astype(o_ref.dtype)

    upd = lax.dot_general(vh, pre["k_out"][:, kl], (((0,), (0,)), ((), ())), preferred_element_type=F32)
    s_ref[h] = st * pre["e_all"][:, kl] + upd


GLA_IN_W = 2 * GLA_QK + GLA_V + LR_PAD


def _proj_gla_bwd_kernel(x0_ref, xn_ref, g_ref, w_ref, wdec, bdec, sign_ref, p_ref, ob_ref, s_ref, even_ref, odd_ref,
                         h_even, h_odd, *, scales, t, nc):
    step = pl.program_id(1)

    @pl.when(step == 0)
    def _():
        s_ref[...] = jnp.zeros_like(s_ref)
        odd_ref[...] = jnp.zeros_like(odd_ref)
        h_odd[...] = _normed_input(x0_ref, g_ref)

    body = functools.partial(_proj_gla_bwd_step, xn_ref, g_ref, w_ref, wdec, bdec, sign_ref, p_ref, ob_ref, s_ref,
                             scales=scales, t=t, nc=nc)
    pl.when(step % 2 == 0)(lambda: body(odd_ref, even_ref, h_odd, h_even))
    pl.when(step % 2 == 1)(lambda: body(even_ref, odd_ref, h_even, h_odd))


def _proj_gla_bwd_step(xn_ref, g_ref, w_ref, wdec, bdec, sign_ref, p_ref, ob_ref, s_ref, pend_ref, next_ref,
                       h_ref, h_next_ref, *, scales, t, nc):
    order = list(reversed(range(nc)))
    rows_of = {c: slice(c * t, (c + 1) * t) for c in order}

    def pending(c, c0, c1):
        return _Window(pend_ref, rows_of[c], slice(c0, c1))

    la = {c: _gla_log_decay(pending(c, 2 * GLA_QK + GLA_V, GLA_IN_W), wdec, bdec, reverse=True) for c in order}
    hx = h_ref[...]
    n_slots = nc * GLA_HEADS
    per_slot = max(1, (len(scales) - 4) // n_slots)
    n_first = len(scales) - per_slot * n_slots
    n_cover = 2
    _project_chunks(hx, w_ref, p_ref, scales[:n_cover], next_ref)
    b = {c: _gla_cumulate(la[c], reverse=True, t=t) for c in order}
    _project_chunks(hx, w_ref, p_ref, scales[n_cover:n_first], next_ref)
    pre = {c: _gla_factors(b[c], pending(c, 0, GLA_QK), pending(c, GLA_QK, 2 * GLA_QK), sign_ref,
                           reverse=True, t=t) for c in order}
    intra = {}
    waiting = None
    for slot, (h, c) in enumerate((h, c) for h in range(GLA_HEADS) for c in order):
        _project_chunks(hx, w_ref, p_ref, scales[n_first + slot * per_slot:n_first + (slot + 1) * per_slot],
                        next_ref)
        att = _gla_head_levels(pre[c], h, reverse=True, t=t)
        if waiting is not None:
            intra[waiting[0]] = _gla_head_pv(waiting[1], pending(waiting[0][0], 2 * GLA_QK, 2 * GLA_QK + GLA_V),
                                             waiting[0][1])
        waiting = ((c, h), att)
    intra[waiting[0]] = _gla_head_pv(waiting[1], pending(waiting[0][0], 2 * GLA_QK, 2 * GLA_QK + GLA_V),
                                     waiting[0][1])
    for c in order:
        o_win = _Window(ob_ref, rows_of[c], slice(0, GLA_V))
        for h in range(GLA_HEADS):
            _gla_head_inter(pre[c], intra[c, h], o_win, s_ref, h)
    h_next_ref[...] = _normed_input(xn_ref, g_ref)


def _proj_gla_bwd(x2, pre_g, w_in_p, wdec, bdec, n_seq, n_tok, t, nc):
    tb = t * nc
    nt = n_tok // tb
    rows = n_seq * n_tok
    assert (P_GQ, P_GK, P_GV) == (0, GLA_QK, 2 * GLA_QK)
    signs = jnp.asarray(_gla_role_signs(t, reverse=True))

    def proj_blk(s, i):
        return (s * nt + nt - 1 - jnp.minimum(i, nt - 1), 0)

    def gla_blk(s, i):
        return (s * nt + nt - 1 - jnp.maximum(i - 1, 0), 0)

    return pl.pallas_call(
        functools.partial(_proj_gla_bwd_kernel, scales=_proj_scales(), t=t, nc=nc),
        grid=(n_seq, nt + 1),
        in_specs=[
            pl.BlockSpec((tb, D_MODEL), lambda s, i: (s * nt + nt - 1, 0), pipeline_mode=pl.Buffered(1)),
            pl.BlockSpec((tb, D_MODEL), lambda s, i: proj_blk(s, i + 1)),
            _resident((1, D_MODEL), lambda s, i: (0, 0)),
            _resident((D_MODEL, P_W), lambda s, i: (0, 0)),
            _resident((LR_PAD, 2 * GLA_QK), lambda s, i: (0, 0)),
            _resident((1, 2 * GLA_QK), lambda s, i: (0, 0)),
            _resident(signs.shape, lambda s, i: (0, 0, 0)),
        ],
        out_specs=[pl.BlockSpec((tb, P_W), proj_blk), pl.BlockSpec((tb, GLA_V), gla_blk)],
        out_shape=[jax.ShapeDtypeStruct((rows, P_W), BF16), jax.ShapeDtypeStruct((rows, GLA_V), BF16)],
        scratch_shapes=[pltpu.VMEM((GLA_HEADS, GLA_DV, GLA_DK), F32),
                        pltpu.VMEM((tb, GLA_IN_W), BF16), pltpu.VMEM((tb, GLA_IN_W), BF16),
                        pltpu.VMEM((tb, D_MODEL), BF16), pltpu.VMEM((tb, D_MODEL), BF16)],
        compiler_params=_cparams(2),
        name="proj_gla_bwd",
    )(x2, x2, pre_g, w_in_p, wdec, bdec, signs)


def _gla_fwd_out_kernel(*refs, t, nc, n_rows):
    s_ref, even_ref, odd_ref = refs[-7:-4]
    step = pl.program_id(1)

    @pl.when(step == 0)
    def _():
        s_ref[...] = jnp.zeros_like(s_ref)
        odd_ref[...] = jnp.zeros_like(odd_ref)

    body = functools.partial(_gla_fwd_out_step, refs[:-7], s_ref, refs[-4:], t=t, nc=nc, n_rows=n_rows)
    pl.when(step % 2 == 0)(lambda: body(odd_ref, even_ref))
    pl.when(step % 2 == 1)(lambda: body(even_ref, odd_ref))


def _gla_fwd_out_step(io_refs, s_ref, work_refs, of_pend, of_new, *, t, nc, n_rows):
    (q_ref, k_ref, v_ref, lr_ref, wdec, bdec, sign_ref, ob_ref, gg_ref,
     nq_ref, nkp_ref, nkc_ref, nkn_ref, nvp_ref, nvc_ref, nvn_ref, ng_ref, bias_ref, mq_ref, mg_ref, mkv_ref,
     x_ref, w_ref, gng_ref, post_ref, y_ref) = io_refs
    kbuf, vbuf, nat_ref, mem_ref = work_refs

    chunk_rows = [slice(c * t, (c + 1) * t) for c in range(nc)]
    la = [_gla_log_decay(_Window(lr_ref, r, slice(0, LR_PAD)), wdec, bdec, reverse=False) for r in chunk_rows]

    acc = None
    for h in range(GLA_HEADS):
        lanes = slice(h * GLA_DV, (h + 1) * GLA_DV)
        o = of_pend[:, lanes].astype(F32) + ob_ref[:, lanes].astype(F32)
        ms = jnp.mean(o * o, axis=-1, keepdims=True)
        on = (o * lax.rsqrt(ms + EPS)) * gng_ref[...]
        og = (on * gg_ref[:, lanes].astype(F32)).astype(BF16)
        part = jnp.dot(og, w_ref[lanes, :], preferred_element_type=F32)
        acc = part if acc is None else acc + part

    b = [_gla_cumulate(la_c, reverse=False, t=t) for la_c in la]
    pre = [_gla_factors(b[c], _Window(q_ref, r, slice(0, GLA_QK)), _Window(k_ref, r, slice(0, GLA_QK)), sign_ref,
                        reverse=False, t=t)
           for c, r in enumerate(chunk_rows)]

    prev_block = jnp.maximum(pl.program_id(1) - 1, 0)
    _nat_fill(nkp_ref, nkc_ref, nkn_ref, nvp_ref, nvc_ref, nvn_ref, kbuf, vbuf)
    nat_rows = list(range(NAT_TILE_ROWS))
    rows_per_part = NAT_TILE_ROWS // (nc * GLA_HEADS)
    intra = {}
    waiting = None

    def finish(item):
        (c, h), att = item
        intra[c, h] = _gla_head_pv(att, _Window(v_ref, chunk_rows[c], slice(0, GLA_V)), h)

    for h in range(GLA_HEADS):
        for c in range(nc):
            for rr in nat_rows[:rows_per_part]:
                _nat_row(prev_block, rr, nq_ref, ng_ref, bias_ref, nat_ref, kbuf, vbuf, n_rows=n_rows)
            del nat_rows[:rows_per_part]
            att = _gla_head_levels(pre[c], h, reverse=False, t=t)
            if waiting is not None:
                finish(waiting)
            waiting = ((c, h), att)
    finish(waiting)
    for rr in nat_rows:
        _nat_row(prev_block, rr, nq_ref, ng_ref, bias_ref, nat_ref, kbuf, vbuf, n_rows=n_rows)
    _memattn_block(mq_ref, mg_ref, mkv_ref, mem_ref)

    acc = acc + jnp.dot(nat_ref[...], w_ref[GLA_V:GLA_V + NAT_W, :], preferred_element_type=F32)
    acc = acc + jnp.dot(mem_ref[...], w_ref[GLA_V + NAT_W:MIX_W, :], preferred_element_type=F32)
    ms = jnp.mean(acc * acc, axis=-1, keepdims=True)
    y_ref[...] = x_ref[...] + (acc * lax.rsqrt(ms + EPS)) * post_ref[...]

    for c, r in enumerate(chunk_rows):
        for h in range(GLA_HEADS):
            _gla_head_inter(pre[c], intra[c, h], _Window(of_new, r, slice(0, GLA_V)), s_ref, h)


def _gla_fwd_out(p, o_b, mkv, bias_tab, x2, wdec, bdec, w_out, gla_ng, post_g, n_seq, n_tok, t, nc):
    tb = t * nc
    assert tb == NAT_TILE, "the attention branches work on 8 grid rows per step"
    nt = n_tok // tb
    rows = n_seq * n_tok
    halo_per_blk = tb // NAT_HALO

    signs = jnp.asarray(_gla_role_signs(t, reverse=False))

    def gla_blk(col):
        return lambda s, i: (s * nt + jnp.minimum(i, nt - 1), col)

    def out_blk(col):
        return lambda s, i: (s * nt + jnp.maximum(i - 1, 0), col)

    def halo_blk(col, side):
        def index(s, i):
            j = jnp.maximum(i - 1, 0) * halo_per_blk
            j = jnp.maximum(j - 1, 0) if side < 0 else jnp.minimum(j + halo_per_blk, nt * halo_per_blk - 1)
            return (s * nt * halo_per_blk + j, col)
        return index

    nat_blk = (tb, NAT_W)
    halo = (NAT_HALO, NAT_W)
    nq_col, nk_col, nv_col, ng_col = (c // NAT_W for c in (P_NQ, P_NK, P_NV, P_NG))

    return pl.pallas_call(
        functools.partial(_gla_fwd_out_kernel, t=t, nc=nc, n_rows=n_tok // GRID_W),
        grid=(n_seq, nt + 1),
        in_specs=[
            pl.BlockSpec((tb, GLA_QK), gla_blk(P_GQ // GLA_QK)),
            pl.BlockSpec((tb, GLA_QK), gla_blk(P_GK // GLA_QK)),
            pl.BlockSpec((tb, GLA_V), gla_blk(P_GV // GLA_V)),
            pl.BlockSpec((tb, LR_PAD), gla_blk(P_LR // LR_PAD)),
            _resident((LR_PAD, 2 * GLA_QK), lambda s, i: (0, 0)),
            _resident((1, 2 * GLA_QK), lambda s, i: (0, 0)),
            _resident(signs.shape, lambda s, i: (0, 0, 0)),
            pl.BlockSpec((tb, GLA_V), out_blk(0)),
            pl.BlockSpec((tb, GLA_V), out_blk(P_GG // GLA_V)),
            pl.BlockSpec(nat_blk, out_blk(nq_col)),
            pl.BlockSpec(halo, halo_blk(nk_col, -1)), pl.BlockSpec(nat_blk, out_blk(nk_col)),
            pl.BlockSpec(halo, halo_blk(nk_col, 1)),
            pl.BlockSpec(halo, halo_blk(nv_col, -1)), pl.BlockSpec(nat_blk, out_blk(nv_col)),
            pl.BlockSpec(halo, halo_blk(nv_col, 1)),
            pl.BlockSpec(nat_blk, out_blk(ng_col)),
            _resident(bias_tab.shape, lambda s, i: (0, 0, 0, 0)),
            pl.BlockSpec((tb, MEM_W), out_blk(P_MQ // MEM_W)),
            pl.BlockSpec((tb, MEM_W), out_blk(P_MG // MEM_W)),
            pl.BlockSpec((N_MEM, MKV_W), lambda s, i: (s, 0)),
            pl.BlockSpec((tb, D_MODEL), out_blk(0)),
            _resident((MIX_W, D_MODEL), lambda s, i: (0, 0)),
            _resident((1, GLA_DV), lambda s, i: (0, 0)),
            _resident((1, D_MODEL), lambda s, i: (0, 0)),
        ],
        out_specs=pl.BlockSpec((tb, D_MODEL), out_blk(0)),
        out_shape=jax.ShapeDtypeStruct((rows, D_MODEL), F32),
        scratch_shapes=[pltpu.VMEM((GLA_HEADS, GLA_DV, GLA_DK), F32),
                        pltpu.VMEM((tb, GLA_V), BF16), pltpu.VMEM((tb, GLA_V), BF16),
                        pltpu.VMEM((NAT_BUF, NAT_W), BF16), pltpu.VMEM((NAT_BUF, 2 * NAT_W), BF16),
                        pltpu.VMEM((tb, NAT_W), BF16), pltpu.VMEM((tb, MEM_W), BF16)],
        compiler_params=_cparams(2),
        name="gla_fwd_out",
    )(p, p, p, p, wdec, bdec, signs, o_b, p, p, p, p, p, p, p, p, p, bias_tab, p, p, mkv, x2, w_out, gla_ng, post_g)


NAT_TILE_ROWS = 8
NAT_TILE = NAT_TILE_ROWS * GRID_W
NAT_KEYS = NAT_KH * GRID_W


NAT_BIAS_ROWS = 2 * NAT_KH - 2
PAIR_W = 2 * NAT_DH


def _nat_bias_table(rpb):
    cq = np.arange(GRID_W)[:, None]
    ck = np.arange(GRID_W)[None, :]
    cs = np.clip(cq - NAT_KW // 2, 0, GRID_W - NAT_KW)
    col_in = (ck >= cs) & (ck < cs + NAT_KW)
    dcol = np.clip(ck - cq + (NAT_KW - 1), 0, 2 * NAT_KW - 2)
    onehot = ((dcol[None] == np.arange(2 * NAT_KW - 1)[:, None, None]) & col_in[None]).astype(np.float32)
    tab = jnp.einsum("hrd,dqk->hrqk", rpb.astype(F32), jnp.asarray(onehot),
                     precision=lax.Precision.HIGHEST)
    tab = tab * LOG2_E + jnp.asarray(np.where(col_in, 0.0, NEG_BIG).astype(np.float32))
    tab = jnp.concatenate([tab[:, :-1], tab[:, 1:]], axis=-1)
    tab = tab.reshape(NAT_PAIRS, 2, NAT_BIAS_ROWS, GRID_W, PAIR_W)
    return tab.transpose(0, 2, 1, 3, 4).reshape(NAT_PAIRS, NAT_BIAS_ROWS, PAIR_W, PAIR_W)


NAT_HALO = (NAT_KH // 2) * GRID_W
NAT_BUF = NAT_TILE + 2 * NAT_HALO


def _nat_fill(kp_ref, kc_ref, kn_ref, vp_ref, vc_ref, vn_ref, kbuf, vbuf):
    kbuf[0:NAT_HALO, :] = kp_ref[...]
    kbuf[NAT_HALO:NAT_HALO + NAT_TILE, :] = kc_ref[...]
    kbuf[NAT_HALO + NAT_TILE:NAT_BUF, :] = kn_ref[...]
    for r0, r1, v_ref in ((0, NAT_HALO, vp_ref), (NAT_HALO, NAT_HALO + NAT_TILE, vc_ref),
                          (NAT_HALO + NAT_TILE, NAT_BUF, vn_ref)):
        ones = jnp.ones((r1 - r0, PAIR_W), BF16)
        for p in range(NAT_PAIRS):
            vbuf[r0:r1, 2 * p * PAIR_W:(2 * p + 1) * PAIR_W] = v_ref[:, p * PAIR_W:(p + 1) * PAIR_W]
            vbuf[r0:r1, (2 * p + 1) * PAIR_W:(2 * p + 2) * PAIR_W] = ones


def _nat_row(j, rr, q_ref, g_ref, bias_ref, o_ref, kbuf, vbuf, *, n_rows):
    lane = lax.broadcasted_iota(jnp.int32, (GRID_W, PAIR_W), 1)
    first = lane < NAT_DH
    row0 = j * NAT_TILE_ROWS
    r = row0 + rr
    rs = jnp.clip(r - NAT_KH // 2, 0, n_rows - NAT_KH)
    d0 = rs - r + (NAT_KH - 1)
    off = pl.multiple_of((rs - row0 + NAT_KH // 2) * GRID_W, GRID_W)
    rows = slice(rr * GRID_W, (rr + 1) * GRID_W)
    scores = []
    for p in range(NAT_PAIRS):
        lanes = slice(p * PAIR_W, (p + 1) * PAIR_W)
        qp = q_ref[rows, lanes]
        zero = jnp.zeros_like(qp)
        q2 = jnp.concatenate([jnp.where(first, qp, zero), jnp.where(first, zero, qp)], axis=0)
        kk = kbuf[pl.ds(off, NAT_KEYS), lanes]
        s = lax.dot_general(q2, kk, (((1,), (1,)), ((), ())), preferred_element_type=F32)
        bias = jnp.concatenate([bias_ref[p, d0 + 2 * jj] for jj in range(NAT_KH // 2)], axis=1)
        scores.append(s + bias)
    probs = []
    for s in scores:
        probs.append(jnp.exp2(s - jnp.max(s, axis=-1, keepdims=True)).astype(BF16))
    for p, e in enumerate(probs):
        lanes = slice(p * PAIR_W, (p + 1) * PAIR_W)
        vv = vbuf[pl.ds(off, NAT_KEYS), 2 * p * PAIR_W:(2 * p + 2) * PAIR_W]
        pv = jnp.dot(e, vv, preferred_element_type=F32)
        pv = pv[:, 0:PAIR_W] / pv[:, PAIR_W:2 * PAIR_W]
        o = jnp.where(first, pv[0:GRID_W, :], pv[GRID_W:2 * GRID_W, :])
        gate = g_ref[rows, lanes].astype(F32)
        o_ref[rows, lanes] = (o * gate).astype(o_ref.dtype)


MKV_W = MEM_W + 2 * MEM_W


def _memkv_kernel(m_ref, g_ref, w_ref, o_ref):
    x = m_ref[...]
    ms = jnp.mean(x * x, axis=-1, keepdims=True)
    h = ((x * lax.rsqrt(ms + EPS)) * g_ref[...]).astype(BF16)
    kv = jnp.dot(h, w_ref[...], preferred_element_type=F32).astype(o_ref.dtype)
    o_ref[:, 0:MEM_W] = kv[:, 0:MEM_W]
    ones = jnp.ones((N_MEM, MEM_DH), o_ref.dtype)
    for hd in range(MEM_HEADS):
        c0 = MEM_W + 2 * hd * MEM_DH
        o_ref[:, c0:c0 + MEM_DH] = kv[:, MEM_W + hd * MEM_DH:MEM_W + (hd + 1) * MEM_DH]
        o_ref[:, c0 + MEM_DH:c0 + 2 * MEM_DH] = ones


def _memkv(mem2, mem_g, w_kv):
    rows = mem2.shape[0]
    return pl.pallas_call(
        _memkv_kernel,
        grid=(rows // N_MEM,),
        in_specs=[
            pl.BlockSpec((N_MEM, D_MODEL), lambda i: (i, 0)),
            _resident((1, D_MODEL), lambda i: (0, 0)),
            _resident((D_MODEL, 2 * MEM_W), lambda i: (0, 0)),
        ],
        out_specs=pl.BlockSpec((N_MEM, MKV_W), lambda i: (i, 0)),
        out_shape=jax.ShapeDtypeStruct((rows, MKV_W), BF16),
        compiler_params=_cparams(1),
        name="memkv",
    )(mem2, mem_g, w_kv)


def _memattn_block(q_ref, g_ref, kv_ref, o_ref):
    for h in range(MEM_HEADS):
        lanes = slice(h * MEM_DH, (h + 1) * MEM_DH)
        kh = kv_ref[:, lanes]
        vh = kv_ref[:, MEM_W + 2 * h * MEM_DH:MEM_W + 2 * (h + 1) * MEM_DH]
        s = lax.dot_general(q_ref[:, lanes], kh, (((1,), (1,)), ((), ())), preferred_element_type=F32)
        e = jnp.exp2(s - jnp.max(s, axis=-1, keepdims=True)).astype(BF16)
        pv = jnp.dot(e, vh, preferred_element_type=F32)
        o = pv[:, 0:MEM_DH] / pv[:, MEM_DH:2 * MEM_DH]
        gate = g_ref[:, lanes].astype(F32)
        o_ref[:, lanes] = (o * gate).astype(o_ref.dtype)


def _choose_tile(n, pref):
    t = pref
    while n % t:
        t //= 2
    return t


IN_W = 2 * GLA_QK + 2 * GLA_V + 2 * GLA_RANK + 4 * NAT_W + 2 * MEM_W
LR0 = 2 * GLA_QK + 2 * GLA_V
REGROUP_ROWS = 128


def _regroup_kernel(w_ref, o_ref):
    w = w_ref[0]
    lr1 = LR0 + 2 * GLA_RANK
    o_ref[:, 0:LR0] = w[:, 0:LR0].astype(BF16)
    o_ref[:, LR0:P_LR] = w[:, lr1:IN_W].astype(BF16)
    tail = jnp.concatenate([w[:, LR0:lr1], jnp.zeros((w.shape[0], LR_PAD - 2 * GLA_RANK), w.dtype)], axis=1)
    o_ref[:, P_LR:P_W] = tail.astype(BF16)


def _regroup_w_in(w_in):
    assert w_in.shape == (1, D_MODEL, IN_W) and P_LR == IN_W - 2 * GLA_RANK
    return pl.pallas_call(
        _regroup_kernel,
        grid=(D_MODEL // REGROUP_ROWS,),
        in_specs=[pl.BlockSpec((1, REGROUP_ROWS, IN_W), lambda i: (0, i, 0))],
        out_specs=pl.BlockSpec((REGROUP_ROWS, P_W), lambda i: (i, 0)),
        out_shape=jax.ShapeDtypeStruct((D_MODEL, P_W), BF16),
        compiler_params=_cparams(1),
        name="regroup_w_in",
    )(w_in)


def _prepare_weights(w_in, gw_f, gb_f, gw_b, gb_b, w_mem_kv, w_out):
    w_in_p = _regroup_w_in(w_in)
    zero = jnp.zeros((GLA_RANK, GLA_QK), gw_f.dtype)
    wdec = jnp.concatenate([jnp.concatenate([gw_f, zero], axis=1), jnp.concatenate([zero, gw_b], axis=1),
                            jnp.zeros((LR_PAD - 2 * GLA_RANK, 2 * GLA_QK), gw_f.dtype)], axis=0)
    bdec = jnp.concatenate([gb_f, gb_b]).reshape(1, 2 * GLA_QK).astype(F32)
    return w_in_p, wdec.astype(BF16), bdec, w_mem_kv.astype(BF16), w_out.astype(BF16)


def _trunk(x, mkv, pre_g, w_in_p, wdec, bdec, gla_ng, bias_tab, w_out, post_g):
    n_seq, n_tok, _ = x.shape
    rows = n_seq * n_tok
    x2 = x.reshape(rows, D_MODEL)
    t_gla = _choose_tile(n_tok, 256)
    n_chunks = 2 if n_tok % (2 * t_gla) == 0 else 1
    p, o_b = _proj_gla_bwd(x2, pre_g, w_in_p, wdec, bdec, n_seq, n_tok, t_gla, n_chunks)
    y = _gla_fwd_out(p, o_b, mkv, bias_tab, x2, wdec, bdec, w_out, gla_ng, post_g, n_seq, n_tok, t_gla, n_chunks)
    return y.reshape(x.shape)


def kernel(x_prompt, x_sample, mem_prompt, mem_sample, pre_norm_g, w_in, gla_w_fwd, gla_b_fwd, gla_w_bwd,
           gla_b_bwd, gla_norm_g, nat_rpb, mem_norm_g, w_mem_kv, w_out, post_norm_g):
    assert pre_norm_g.shape[0] == 1, "single-layer trunk"
    w_in_p, wdec, bdec, w_kv, w_o = _prepare_weights(
        w_in, gla_w_fwd[0], gla_b_fwd[0], gla_w_bwd[0], gla_b_bwd[0], w_mem_kv[0], w_out[0])
    pre_g = pre_norm_g[0].reshape(1, D_MODEL)
    post_g = post_norm_g[0].reshape(1, D_MODEL)
    gla_ng = gla_norm_g[0].reshape(1, GLA_DV)
    mem_g = mem_norm_g[0].reshape(1, D_MODEL)
    bias_tab = _nat_bias_table(nat_rpb[0])

    n_p = mem_prompt.shape[0]
    mem_all = jnp.concatenate([mem_prompt, mem_sample], axis=0).reshape(-1, D_MODEL)
    mkv = _memkv(mem_all, mem_g, w_kv)
    mkv_p, mkv_s = mkv[:n_p * N_MEM], mkv[n_p * N_MEM:]

    run = functools.partial(_trunk, pre_g=pre_g, w_in_p=w_in_p, wdec=wdec, bdec=bdec, gla_ng=gla_ng,
                            bias_tab=bias_tab, w_out=w_o, post_g=post_g)
    return (run(x_prompt, mkv_p), run(x_sample, mkv_s))
```
